```python
import jax, jax.numpy as jnp
from jax import lax
import numpy as np

D_MODEL = 1024
BATCH = 4
SEQ = 4096
DEPTH = 4
DEC_BATCH = 128
DEC_SEQ = 1
PAST_LEN = 2048
PAGE_SIZE = 128

H_RET = 4
DK_RET = 64
DV_RET = 128
RET_CHUNK = 128
RET_THETA = 10000.0
H_ATT = 8
HD_ATT = 64
BRANCHES = ((128, 1), (512, 4), (2048, 16))
WINDOW_MAX = 2048
Q_BLOCK = 128
ROPE_THETA = 10000.0
W_RET_QK = H_RET * DK_RET
W_RET_V = H_RET * DV_RET
W_ATT = H_ATT * HD_ATT
MIX_WIDTH = W_RET_V + W_ATT
IN_WIDTH = 2 * W_RET_QK + 2 * W_RET_V + 3 * W_ATT
D_FF = -(-(8 * D_MODEL) // (3 * 256)) * 256
NORM_EPS = 1e-6

kernel_name = 'hybrid_retention_dilated_attention_decode_step'


def _block(length, size):
    return size if length % size == 0 else length


def rmsnorm(x, g):
    xf = x.astype(jnp.float32)
    y = xf * lax.rsqrt(jnp.mean(xf * xf, axis=-1, keepdims=True) + NORM_EPS)
    return (y * g.astype(jnp.float32)).astype(x.dtype)


def rope(x, pos):
    half = HD_ATT // 2
    inv = ROPE_THETA ** (-2.0 * jnp.arange(half, dtype=jnp.float32) / HD_ATT)
    ang = pos[:, None] * inv[None, :]
    cos = jnp.cos(ang)[:, None, :]
    sin = jnp.sin(ang)[:, None, :]
    xf = x.astype(jnp.float32)
    x1, x2 = xf[..., :half], xf[..., half:]
    return jnp.concatenate([x1 * cos - x2 * sin, x2 * cos + x1 * sin], axis=-1).astype(x.dtype)


def ret_rotate(x, pos):
    angle = 1.0 / (RET_THETA ** jnp.linspace(0.0, 1.0, DK_RET // 2, dtype=jnp.float32))
    angle = jnp.repeat(angle, 2)
    ang = pos[:, None] * angle[None, :]
    x1 = x[..., ::2]
    x2 = x[..., 1::2]
    rot = jnp.stack([-x2, x1], axis=-1).reshape(x.shape)
    return x * jnp.cos(ang) + rot * jnp.sin(ang)


def retention(q, k, v, state0, log_gamma):
    B, H, L, _ = q.shape
    C = _block(L, RET_CHUNK)
    N = L // C

    def chunks(t):
        return t.reshape(B, H, N, C, t.shape[-1]).transpose(2, 0, 1, 3, 4)

    idx = jnp.arange(C, dtype=jnp.float32)
    diff = idx[:, None] - idx[None, :]
    decay_mask = jnp.where(diff >= 0, jnp.exp(jnp.maximum(diff, 0.0) * log_gamma[:, None, None]), 0.0)
    q_decay = jnp.exp((idx + 1.0)[None, :] * log_gamma[:, None])[:, :, None]
    k_decay = jnp.exp((C - 1.0 - idx)[None, :] * log_gamma[:, None])[:, :, None]
    chunk_decay = jnp.exp(C * log_gamma)[:, None, None]

    def step(R, xs):
        qc, kc, vc = xs
        scores = jnp.einsum('bhid,bhjd->bhij', qc, kc) * decay_mask
        o = jnp.einsum('bhij,bhje->bhie', scores, vc) + jnp.einsum('bhid,bhde->bhie', qc, R) * q_decay
        R = R * chunk_decay + jnp.einsum('bhjd,bhje->bhde', kc * k_decay, vc)
        return R, o

    R, o = lax.scan(step, state0, (chunks(q), chunks(k), chunks(v)))
    o = o.transpose(1, 2, 0, 3, 4).reshape(B, H, L, v.shape[-1])
    return o, R


def group_norm(y, g):
    mu = jnp.mean(y, axis=-1, keepdims=True)
    var = jnp.mean(jnp.square(y - mu), axis=-1, keepdims=True)
    yn = (y - mu) * lax.rsqrt(var + NORM_EPS)
    B, H, L, DV = y.shape
    return yn.transpose(0, 2, 1, 3).reshape(B, L, H * DV) * g.astype(jnp.float32)


def dilated_attention(q, k_ext, v_ext, base):
    B, L, H, D = q.shape
    Qb = _block(L, Q_BLOCK)
    nb = L // Qb

    def block_fn(b):
        start = b * Qb
        qs = lax.dynamic_slice_in_dim(q, start, Qb, axis=1)
        qi = start + jnp.arange(Qb)
        outs, lses = [], []
        for w, d in BRANCHES:
            m = jnp.arange(w // d + 1)
            idx = base + qi[:, None] - d * m[None, :]
            valid = idx >= 0
            flat = jnp.maximum(idx, 0).reshape(-1)
            kg = jnp.take(k_ext, flat, axis=1).reshape(B, Qb, -1, H, D)
            vg = jnp.take(v_ext, flat, axis=1).reshape(B, Qb, -1, H, D)
            s = jnp.einsum('bqhd,bqmhd->bhqm', qs, kg, preferred_element_type=jnp.float32)
            s = jnp.where(valid[None, None], s, -jnp.inf)
            lse = jax.nn.logsumexp(s, axis=-1)
            p = jnp.exp(s - lse[..., None])
            outs.append(jnp.einsum('bhqm,bqmhd->bhqd', p.astype(vg.dtype), vg, preferred_element_type=jnp.float32))
            lses.append(lse)
        wts = jax.nn.softmax(jnp.stack(lses), axis=0)
        o = jnp.sum(wts[..., None] * jnp.stack(outs), axis=0)
        return o.transpose(0, 2, 1, 3).astype(q.dtype)

    out = lax.map(block_fn, jnp.arange(nb))
    return out.transpose(1, 0, 2, 3, 4).reshape(B, L, H * D)


def hybrid_mixer(xn, pos, ret_state, k_past, v_past, w_in, w_gn, w_o):
    B, L, _ = xn.shape
    offs = list(np.cumsum([W_RET_QK, W_RET_QK, W_RET_V, W_RET_V, W_ATT, W_ATT]))
    q_r, k_r, v_r, g_r, q_a, k_a, v_a = jnp.split(xn @ w_in, [int(o) for o in offs], axis=-1)

    def ret_heads(t, hd):
        return t.reshape(B, L, H_RET, hd).transpose(0, 2, 1, 3).astype(jnp.float32)
    qr = ret_rotate(ret_heads(q_r, DK_RET), pos)
    kr = ret_rotate(ret_heads(k_r, DK_RET), pos) * (DK_RET ** -0.5)
    vr = ret_heads(v_r, DV_RET)
    log_gamma = jnp.log1p(-jnp.exp2(-5.0 - jnp.arange(H_RET, dtype=jnp.float32)))
    state0 = jnp.zeros((B, H_RET, DK_RET, DV_RET), jnp.float32) if ret_state is None else ret_state.astype(jnp.float32)
    yr, new_state = retention(qr, kr, vr, state0, log_gamma)
    y_ret = (jax.nn.silu(g_r.astype(jnp.float32)) * group_norm(yr, w_gn)).astype(xn.dtype)

    qa = rope(q_a.reshape(B, L, H_ATT, HD_ATT), pos) * jnp.asarray(HD_ATT ** -0.5, xn.dtype)
    ka = rope(k_a.reshape(B, L, H_ATT, HD_ATT), pos)
    va = v_a.reshape(B, L, H_ATT, HD_ATT)
    if k_past is None:
        k_ext, v_ext, base = ka, va, 0
    else:
        k_ext = jnp.concatenate([k_past.astype(ka.dtype), ka], axis=1)
        v_ext = jnp.concatenate([v_past.astype(va.dtype), va], axis=1)
        base = k_past.shape[1]
    y_att = dilated_attention(qa, k_ext, v_ext, base)

    out = jnp.concatenate([y_ret, y_att], axis=-1) @ w_o
    return out, new_state, ka, va


def decoder_layer(x, pos, ret_state, k_past, v_past, g1, w_in, w_gn, w_o, g2, w_gate, w_up, w_down):
    mix, R, ka, va = hybrid_mixer(rmsnorm(x, g1), pos, ret_state, k_past, v_past, w_in, w_gn, w_o)
    x = x + mix
    xn = rmsnorm(x, g2)
    x = x + (jax.nn.silu(xn @ w_gate) * (xn @ w_up)) @ w_down
    return x, R, ka, va


def setup_inputs(seed: int = 0) -> dict:
    key = jax.random.key(seed)
    ks = jax.random.split(key, 16)
    f32 = jnp.float32
    w_buf = min(WINDOW_MAX, PAST_LEN)
    return {
        'x_prompt': jax.random.normal(ks[0], (BATCH, SEQ, D_MODEL), f32),
        'x_sample': jax.random.normal(ks[1], (DEC_BATCH, DEC_SEQ, D_MODEL), f32),
        'state_ret': jax.random.normal(ks[2], (DEPTH, DEC_BATCH, H_RET, DK_RET, DV_RET), f32),
        'cache_k': jax.random.normal(ks[3], (DEPTH, DEC_BATCH, w_buf, H_ATT, HD_ATT), f32),
        'cache_v': jax.random.normal(ks[4], (DEPTH, DEC_BATCH, w_buf, H_ATT, HD_ATT), f32),
        'norm1_g': 1.0 + 0.02 * jax.random.normal(ks[5], (DEPTH, D_MODEL), f32),
        'w_in': jax.random.normal(ks[6], (DEPTH, D_MODEL, IN_WIDTH), f32) * D_MODEL ** -0.5,
        'w_gn': 1.0 + 0.02 * jax.random.normal(ks[7], (DEPTH, W_RET_V), f32),
        'w_o': jax.random.normal(ks[8], (DEPTH, MIX_WIDTH, D_MODEL), f32) * MIX_WIDTH ** -0.5,
        'norm2_g': 1.0 + 0.02 * jax.random.normal(ks[9], (DEPTH, D_MODEL), f32),
        'w_gate': jax.random.normal(ks[10], (DEPTH, D_MODEL, D_FF), f32) * D_MODEL ** -0.5,
        'w_up': jax.random.normal(ks[11], (DEPTH, D_MODEL, D_FF), f32) * D_MODEL ** -0.5,
        'w_down': jax.random.normal(ks[12], (DEPTH, D_FF, D_MODEL), f32) * D_FF ** -0.5,
        'final_g': 1.0 + 0.02 * jax.random.normal(ks[13], (D_MODEL,), f32),
    }


def reference(x_prompt, x_sample, state_ret, cache_k, cache_v, norm1_g, w_in, w_gn, w_o, norm2_g, w_gate, w_up, w_down, final_g):
    L_p = x_prompt.shape[1]
    L_s = x_sample.shape[1]
    pos_p = jnp.arange(L_p, dtype=jnp.float32)
    pos_s = PAST_LEN + jnp.arange(L_s, dtype=jnp.float32)
    keep_p = min(WINDOW_MAX, L_p)
    xp, xs = x_prompt, x_sample
    st_p, kp_l, vp_l, st_s, ks_l, vs_l = [], [], [], [], [], []
    for l in range(DEPTH):
        params = (norm1_g[l], w_in[l], w_gn[l], w_o[l], norm2_g[l], w_gate[l], w_up[l], w_down[l])
        xp, Rp, kp, vp = decoder_layer(xp, pos_p, None, None, None, *params)
        xs, Rs, ksn, vsn = decoder_layer(xs, pos_s, state_ret[l], cache_k[l], cache_v[l], *params)
        st_p.append(Rp)
        kp_l.append(kp[:, L_p - keep_p:])
        vp_l.append(vp[:, L_p - keep_p:])
        st_s.append(Rs)
        ks_l.append(ksn)
        vs_l.append(vsn)
    y_prompt = rmsnorm(xp, final_g)
    y_sample = rmsnorm(xs, final_g)
    state_ret_prompt = jnp.stack(st_p)
    cache_k_prompt = jnp.stack(kp_l)
    cache_v_prompt = jnp.stack(vp_l)
    state_ret_sample = jnp.stack(st_s)
    cache_k_sample = jnp.stack(ks_l)
    cache_v_sample = jnp.stack(vs_l)
    return (y_prompt, y_sample, state_ret_prompt, cache_k_prompt, cache_v_prompt, state_ret_sample, cache_k_sample, cache_v_sample)
```

```python
import functools

import jax
import jax.numpy as jnp
import numpy as np
from jax import lax
from jax.experimental import pallas as pl
from jax.experimental.pallas import tpu as pltpu

F32 = jnp.float32
BF16 = jnp.bfloat16

D_MODEL = 1024
H_RET = 4
DK_RET = 64
DV_RET = 128
RET_CHUNK = 128
RET_THETA = 10000.0
H_ATT = 8
HD_ATT = 64
BRANCHES = ((128, 1), (512, 4), (2048, 16))
WINDOW_MAX = 2048
ROPE_THETA = 10000.0
W_RET_QK = H_RET * DK_RET
W_RET_V = H_RET * DV_RET
W_ATT = H_ATT * HD_ATT
IN_WIDTH = 2 * W_RET_QK + 2 * W_RET_V + 3 * W_ATT
D_FF = 2816
NORM_EPS = 1e-6

LANES = 128
Q_TILE = 128
VMEM_LIMIT = 56 * 1024 * 1024

FF_CHUNKS = ((0, 1024), (1024, 1024), (2048, 768))


def _cparams(n_axes):
    return pltpu.CompilerParams(
        dimension_semantics=("arbitrary",) * n_axes, vmem_limit_bytes=VMEM_LIMIT
    )


def _rms(x, g):
    return (x * lax.rsqrt(jnp.mean(x * x, axis=-1, keepdims=True) + NORM_EPS)) * g


def _silu(x):
    return x * (1.0 / (1.0 + jnp.exp(-x)))


def _dot(a, b):
    return jnp.dot(a, b, preferred_element_type=F32)


def _dot_nt(a, b):
    return lax.dot_general(a, b, (((1,), (1,)), ((), ())), preferred_element_type=F32)


def _dot_tn(a, b):
    return lax.dot_general(a, b, (((0,), (0,)), ((), ())), preferred_element_type=F32)


def _ret_tables(pos):
    angle = 1.0 / (RET_THETA ** jnp.linspace(0.0, 1.0, DK_RET // 2, dtype=F32))
    angle = jnp.repeat(angle, 2)
    ang = pos[:, None] * angle[None, :]
    cos, sin = jnp.cos(ang), jnp.sin(ang)
    even = (jnp.arange(DK_RET) % 2) == 0
    ssin = jnp.where(even[None, :], -sin, sin)
    return jnp.tile(cos, (1, 2)), jnp.tile(ssin, (1, 2))


def _rope_tables(pos):
    half = HD_ATT // 2
    inv = ROPE_THETA ** (-2.0 * jnp.arange(half, dtype=F32) / HD_ATT)
    ang = pos[:, None] * inv[None, :]
    cos, sin = jnp.cos(ang), jnp.sin(ang)
    cosf = jnp.concatenate([cos, cos], axis=-1)
    ssin = jnp.concatenate([-sin, sin], axis=-1)
    return jnp.tile(cosf, (1, 2)), jnp.tile(ssin, (1, 2))


def _decay_tables(C):
    log_gamma = jnp.log1p(-jnp.exp2(-5.0 - jnp.arange(H_RET, dtype=F32)))
    idx = jnp.arange(C, dtype=F32)
    diff = idx[:, None] - idx[None, :]
    dmask = jnp.where(diff >= 0, jnp.exp(jnp.maximum(diff, 0.0) * log_gamma[:, None, None]), 0.0)
    q_decay = jnp.exp((idx + 1.0)[None, :] * log_gamma[:, None])
    k_decay = jnp.exp((C - 1.0 - idx)[None, :] * log_gamma[:, None])
    chunk_decay = jnp.exp(C * log_gamma)
    qdec = jnp.broadcast_to(q_decay[:, :, None], (H_RET, C, DV_RET))
    kdec = jnp.repeat(k_decay.T, DK_RET, axis=1)
    cdec = jnp.broadcast_to(jnp.repeat(chunk_decay, DK_RET)[:, None], (W_RET_QK, DV_RET))
    return dmask, qdec, kdec, cdec


def _in_proj_kernel(x_ref, g_ref, w_ref, rc_ref, rs_ref, ac_ref, as_ref,
                    qr_ref, kr_ref, vr_ref, gr_ref, qa_ref, ka_ref, va_ref):
    xn = _rms(x_ref[...], g_ref[...]).astype(BF16)
    lane = lax.broadcasted_iota(jnp.int32, (1, LANES), 1)
    even = (lane & 1) == 0
    low = (lane & (HD_ATT - 1)) < HD_ATT // 2

    def proj(c0, n):
        return _dot(xn, w_ref[:, c0:c0 + n])

    def rot_pairs(y):
        partner = jnp.where(even, pltpu.roll(y, LANES - 1, 1), pltpu.roll(y, 1, 1))
        return y * rc_ref[...] + partner * rs_ref[...]

    def rot_half(y):
        h = HD_ATT // 2
        partner = jnp.where(low, pltpu.roll(y, LANES - h, 1), pltpu.roll(y, h, 1))
        return y * ac_ref[...] + partner * as_ref[...]

    y = proj(0, 2 * W_RET_QK)
    for c in range(W_RET_QK // LANES):
        sl = slice(c * LANES, (c + 1) * LANES)
        qr_ref[:, sl] = rot_pairs(y[:, sl]).astype(qr_ref.dtype)
        k = rot_pairs(y[:, W_RET_QK + c * LANES:W_RET_QK + (c + 1) * LANES]) * (DK_RET ** -0.5)
        kr_ref[:, sl] = k.astype(kr_ref.dtype)
    c0 = 2 * W_RET_QK
    vr_ref[...] = proj(c0, W_RET_V).astype(vr_ref.dtype)
    c0 += W_RET_V
    gr_ref[...] = proj(c0, W_RET_V).astype(gr_ref.dtype)
    c0 += W_RET_V
    y = proj(c0, W_ATT)
    for c in range(W_ATT // LANES):
        sl = slice(c * LANES, (c + 1) * LANES)
        qa_ref[:, sl] = rot_half(y[:, sl]) * (HD_ATT ** -0.5)
    c0 += W_ATT
    y = proj(c0, W_ATT)
    for c in range(W_ATT // LANES):
        sl = slice(c * LANES, (c + 1) * LANES)
        ka_ref[:, sl] = rot_half(y[:, sl])
    c0 += W_ATT
    va_ref[...] = proj(c0, W_ATT)


def _in_proj(x, g, w, tabs, *, tm, tab_blocks, act_dtype):
    n = x.shape[0]
    grid = (n // tm,)
    row = lambda i: (i, 0)
    const = lambda i: (0, 0)
    tab = lambda i: (i % tab_blocks, 0)
    outs = [
        (W_RET_QK, act_dtype), (W_RET_QK, act_dtype), (W_RET_V, act_dtype), (W_RET_V, act_dtype),
        (W_ATT, F32), (W_ATT, F32), (W_ATT, F32),
    ]
    return pl.pallas_call(
        _in_proj_kernel,
        grid=grid,
        in_specs=[
            pl.BlockSpec((tm, D_MODEL), row),
            pl.BlockSpec((1, D_MODEL), const),
            pl.BlockSpec((D_MODEL, IN_WIDTH), const),
        ] + [pl.BlockSpec((tm, LANES), tab)] * 4,
        out_specs=[pl.BlockSpec((tm, wd), row) for wd, _ in outs],
        out_shape=[jax.ShapeDtypeStruct((n, wd), dt) for wd, dt in outs],
        compiler_params=_cparams(1),
        name="in_proj",
    )(x, g.reshape(1, D_MODEL), w, *tabs)


def _group_norm_gate(o, gate, wgn):
    mu = jnp.mean(o, axis=-1, keepdims=True)
    var = jnp.mean(jnp.square(o - mu), axis=-1, keepdims=True)
    yn = (o - mu) * lax.rsqrt(var + NORM_EPS)
    return _silu(gate) * (yn * wgn)


def _ret_prompt_kernel(q_ref, k_ref, v_ref, g_ref, wgn_ref, dmask_ref, qdec_ref, kdec_ref, cdec_ref,
                       y_ref, st_ref, r_ref, *, n_chunks):
    t = pl.program_id(1)

    @pl.when(t == 0)
    def _():
        r_ref[...] = jnp.zeros_like(r_ref)

    head_of_lane = lax.broadcasted_iota(jnp.int32, (1, W_RET_QK), 1) // DK_RET
    C = RET_CHUNK
    for c in range(n_chunks):
        rows = slice(c * C, (c + 1) * C)
        q = q_ref[rows, :]
        k = k_ref[rows, :]
        v = v_ref[rows, :]
        qm = jnp.concatenate(
            [jnp.where(head_of_lane == h, q, jnp.zeros_like(q)) for h in range(H_RET)], axis=0)
        s_all = _dot_nt(qm, k)
        x_all = _dot(qm, r_ref[...].astype(BF16))
        for h in range(H_RET):
            hr = slice(h * C, (h + 1) * C)
            hv = slice(h * DV_RET, (h + 1) * DV_RET)
            s = s_all[hr] * dmask_ref[h]
            o = _dot(s.astype(BF16), v[:, hv]) + x_all[hr] * qdec_ref[h]
            y = _group_norm_gate(o, g_ref[rows, hv].astype(F32), wgn_ref[:, hv])
            y_ref[rows, hv] = y.astype(y_ref.dtype)
        kd = (k.astype(F32) * kdec_ref[...]).astype(BF16)
        kv = _dot_tn(kd, v)
        for h in range(H_RET):
            hk = slice(h * DK_RET, (h + 1) * DK_RET)
            hv = slice(h * DV_RET, (h + 1) * DV_RET)
            r_ref[hk, :] = r_ref[hk, :] * cdec_ref[hk, :] + kv[hk, hv]

    @pl.when(t == pl.num_programs(1) - 1)
    def _():
        st_ref[...] = r_ref[...]


def _ret_prompt(qr, kr, vr, gr, wgn, batch, seq, *, tt):
    dmask, qdec, kdec, cdec = _decay_tables(RET_CHUNK)
    nt = seq // tt
    row = lambda b, t: (b * nt + t, 0)
    c2 = lambda b, t: (0, 0)
    c3 = lambda b, t: (0, 0, 0)
    return pl.pallas_call(
        functools.partial(_ret_prompt_kernel, n_chunks=tt // RET_CHUNK),
        grid=(batch, nt),
        in_specs=[
            pl.BlockSpec((tt, W_RET_QK), row),
            pl.BlockSpec((tt, W_RET_QK), row),
            pl.BlockSpec((tt, W_RET_V), row),
            pl.BlockSpec((tt, W_RET_V), row),
            pl.BlockSpec((1, W_RET_V), c2),
            pl.BlockSpec((H_RET, RET_CHUNK, RET_CHUNK), c3),
            pl.BlockSpec((H_RET, RET_CHUNK, DV_RET), c3),
            pl.BlockSpec((RET_CHUNK, W_RET_QK), c2),
            pl.BlockSpec((W_RET_QK, DV_RET), c2),
        ],
        out_specs=[
            pl.BlockSpec((tt, W_RET_V), row),
            pl.BlockSpec((None, W_RET_QK, DV_RET), lambda b, t: (b, 0, 0)),
        ],
        out_shape=[
            jax.ShapeDtypeStruct((batch * seq, W_RET_V), BF16),
            jax.ShapeDtypeStruct((batch, W_RET_QK, DV_RET), F32),
        ],
        scratch_shapes=[pltpu.VMEM((W_RET_QK, DV_RET), F32)],
        compiler_params=_cparams(2),
        name="ret_prompt",
    )(qr, kr, vr, gr, wgn.reshape(1, W_RET_V), dmask, qdec, kdec, cdec)


def _ret_sample_kernel(q_ref, k_ref, v_ref, g_ref, st_ref, wgn_ref, dmask_ref, qdec_ref, kdec_ref, cdec_ref,
                       y_ref, so_ref, *, sb):
    head_row = lax.broadcasted_iota(jnp.int32, (H_RET, W_RET_QK), 0)
    head_lane = lax.broadcasted_iota(jnp.int32, (H_RET, W_RET_QK), 1) // DK_RET
    own = head_row == head_lane
    for s in range(sb):
        qm = jnp.where(own, q_ref[s:s + 1, :], 0.0)
        km = jnp.where(own, k_ref[s:s + 1, :], 0.0)
        r = st_ref[s]
        x = _dot(qm.astype(BF16), r.astype(BF16))
        sc = jnp.sum(qm * km, axis=-1, keepdims=True) * dmask_ref[...]
        v4 = v_ref[s]
        o = sc * v4 + x * qdec_ref[...]
        y_ref[s] = _group_norm_gate(o, g_ref[s], wgn_ref[...])
        kv = _dot_tn((km * kdec_ref[...]).astype(BF16), v4.astype(BF16))
        so_ref[s] = r * cdec_ref[...] + kv


def _ret_sample(qr, kr, vr, gr, state, wgn, *, sb):
    b = qr.shape[0]
    dmask, qdec, kdec, cdec = _decay_tables(1)
    row = lambda i: (i, 0)
    row3 = lambda i: (i, 0, 0)
    c2 = lambda i: (0, 0)
    y, so = pl.pallas_call(
        functools.partial(_ret_sample_kernel, sb=sb),
        grid=(b // sb,),
        in_specs=[
            pl.BlockSpec((sb, W_RET_QK), row),
            pl.BlockSpec((sb, W_RET_QK), row),
            pl.BlockSpec((sb, H_RET, DV_RET), row3),
            pl.BlockSpec((sb, H_RET, DV_RET), row3),
            pl.BlockSpec((sb, W_RET_QK, DV_RET), row3),
            pl.BlockSpec((H_RET, DV_RET), c2),
            pl.BlockSpec((H_RET, 1), c2),
            pl.BlockSpec((H_RET, DV_RET), c2),
            pl.BlockSpec((1, W_RET_QK), c2),
            pl.BlockSpec((W_RET_QK, DV_RET), c2),
        ],
        out_specs=[
            pl.BlockSpec((sb, H_RET, DV_RET), row3),
            pl.BlockSpec((sb, W_RET_QK, DV_RET), row3),
        ],
        out_shape=[
            jax.ShapeDtypeStruct((b, H_RET, DV_RET), F32),
            jax.ShapeDtypeStruct((b, W_RET_QK, DV_RET), F32),
        ],
        compiler_params=_cparams(1),
        name="ret_sample",
    )(qr, kr, vr.reshape(b, H_RET, DV_RET), gr.reshape(b, H_RET, DV_RET), state,
      wgn.reshape(H_RET, DV_RET), dmask.reshape(H_RET, 1), qdec.reshape(H_RET, DV_RET), kdec, cdec)
    return y.reshape(b, W_RET_V), so


def _att_prompt_kernel(q_ref, k_ref, v_ref, o_ref, kc_ref, vc_ref, m_ref, l_ref, acc_ref, *, seq, keep):
    T = Q_TILE
    head0 = lax.broadcasted_iota(jnp.int32, (1, LANES), 1) < HD_ATT
    qi = lax.broadcasted_iota(jnp.int32, (2 * T, 2 * T), 0) & (T - 1)
    kj = lax.broadcasted_iota(jnp.int32, (2 * T, 2 * T), 1)
    rel = qi - kj

    for bi, (w, d) in enumerate(BRANCHES):
        assert w // d == T
        nblk = seq // (T * d)

        def body(it, carry, d=d, nblk=nblk, first=(bi == 0)):
            r = it // nblk
            j = it % nblk
            qstart = r + d * T * j
            kstart = r + d * T * jnp.maximum(j - 1, 0)
            shift = jnp.where(j > 0, T, 0)
            q = q_ref[pl.ds(qstart, T, stride=d), :].astype(BF16)
            kk = k_ref[pl.ds(kstart, 2 * T, stride=d), :].astype(BF16)
            vv = v_ref[pl.ds(kstart, 2 * T, stride=d), :].astype(BF16)
            zero = jnp.zeros_like(q)
            q2 = jnp.concatenate([jnp.where(head0, q, zero), jnp.where(head0, zero, q)], axis=0)
            s = _dot_nt(q2, kk)
            dist = rel + shift
            s = jnp.where((dist >= 0) & (dist <= T), s, -jnp.inf)
            mb = jnp.max(s, axis=-1, keepdims=True)
            p = jnp.exp(s - mb)
            lb = jnp.sum(p, axis=-1, keepdims=True)
            pv = _dot(p.astype(BF16), vv)
            acc_b = jnp.where(head0, pv[:T], pv[T:])
            m_b = jnp.where(head0, mb[:T], mb[T:])
            l_b = jnp.where(head0, lb[:T], lb[T:])
            rows = pl.ds(qstart, T, stride=d)
            if first:
                m_ref[rows, :] = m_b
                l_ref[rows, :] = l_b
                acc_ref[rows, :] = acc_b
            else:
                m_old = m_ref[rows, :]
                m_new = jnp.maximum(m_old, m_b)
                a = jnp.exp(m_old - m_new)
                b = jnp.exp(m_b - m_new)
                m_ref[rows, :] = m_new
                l_ref[rows, :] = l_ref[rows, :] * a + l_b * b
                acc_ref[rows, :] = acc_ref[rows, :] * a + acc_b * b
            return carry

        lax.fori_loop(0, d * nblk, body, 0)

    def fin(i, carry):
        rows = pl.ds(pl.multiple_of(i * T, T), T)
        o_ref[rows, :] = (acc_ref[rows, :] / l_ref[rows, :]).astype(o_ref.dtype)
        return carry

    lax.fori_loop(0, seq // T, fin, 0)
    kc_ref[...] = k_ref[seq - keep:, :]
    vc_ref[...] = v_ref[seq - keep:, :]


def _att_prompt(qa, ka, va, batch, seq):
    keep = min(WINDOW_MAX, seq)
    nhp = W_ATT // LANES
    blk = lambda b, h: (b, h)
    return pl.pallas_call(
        functools.partial(_att_prompt_kernel, seq=seq, keep=keep),
        grid=(batch, nhp),
        in_specs=[pl.BlockSpec((seq, LANES), blk)] * 3,
        out_specs=[
            pl.BlockSpec((seq, LANES), blk),
            pl.BlockSpec((keep, LANES), blk),
            pl.BlockSpec((keep, LANES), blk),
        ],
        out_shape=[
            jax.ShapeDtypeStruct((batch * seq, W_ATT), BF16),
            jax.ShapeDtypeStruct((batch * keep, W_ATT), F32),
            jax.ShapeDtypeStruct((batch * keep, W_ATT), F32),
        ],
        scratch_shapes=[pltpu.VMEM((seq, LANES), F32)] * 3,
        compiler_params=_cparams(2),
        name="att_prompt",
    )(qa, ka, va)


def _att_sample_kernel(q_ref, kn_ref, vn_ref, cnt_ref, kt_ref, vt_ref, o_ref, *, sb):
    i = pl.program_id(0)
    head_row = lax.broadcasted_iota(jnp.int32, (H_ATT, W_ATT), 0)
    head_lane = lax.broadcasted_iota(jnp.int32, (H_ATT, W_ATT), 1) // HD_ATT
    own = head_row == head_lane
    cnt = cnt_ref[...]
    live = cnt > 0.0
    n_br = float(len(BRANCHES))
    for s in range(sb):
        row = pl.ds(i * sb + s, 1)
        qm = jnp.where(own, q_ref[row, :], 0.0)
        s_new = jnp.sum(qm * kn_ref[row, :], axis=-1, keepdims=True)
        sc = _dot(qm.astype(BF16), kt_ref[s].astype(BF16))
        sc = jnp.where(live, sc, -jnp.inf)
        m = jnp.maximum(jnp.max(sc, axis=-1, keepdims=True), s_new)
        p = cnt * jnp.exp(sc - m)
        p_new = n_br * jnp.exp(s_new - m)
        den = jnp.sum(p, axis=-1, keepdims=True) + p_new
        acc = _dot_nt(p.astype(BF16), vt_ref[s].astype(BF16)) + p_new * vn_ref[row, :]
        out = jnp.where(own, acc / den, 0.0)
        o_ref[row, :] = jnp.sum(out, axis=0, keepdims=True)


def _att_sample(qa, ka, va, kt, vt, layer, *, sb):
    _, b, _, past = kt.shape
    dist = past - np.arange(past)
    cnt = sum(((dist % d == 0) & (dist <= w)).astype(np.float32) for w, d in BRANCHES)
    cnt = jnp.asarray(cnt.reshape(1, past))
    whole = pl.BlockSpec((b, W_ATT), lambda i: (0, 0))
    cache = pl.BlockSpec((None, sb, W_ATT, past), lambda i: (layer, i, 0, 0))
    return pl.pallas_call(
        functools.partial(_att_sample_kernel, sb=sb),
        grid=(b // sb,),
        in_specs=[whole, whole, whole, pl.BlockSpec((1, past), lambda i: (0, 0)), cache, cache],
        out_specs=whole,
        out_shape=jax.ShapeDtypeStruct((b, W_ATT), F32),
        compiler_params=_cparams(1),
        name="att_sample",
    )(qa, ka, va, cnt, kt, vt)


def _out_ffn_kernel(x_ref, yr_ref, ya_ref, wo_ref, g2_ref, wg_ref, wu_ref, wd_ref, gf_ref, o_ref, *, final):
    mix = (_dot(yr_ref[...].astype(BF16), wo_ref[:W_RET_V, :])
           + _dot(ya_ref[...].astype(BF16), wo_ref[W_RET_V:, :]))
    x1 = x_ref[...] + mix
    xn = _rms(x1, g2_ref[...]).astype(BF16)
    acc = None
    for c0, n in FF_CHUNKS:
        gate = _dot(xn, wg_ref[:, c0:c0 + n])
        up = _dot(xn, wu_ref[:, c0:c0 + n])
        part = _dot((_silu(gate) * up).astype(BF16), wd_ref[c0:c0 + n, :])
        acc = part if acc is None else acc + part
    x2 = x1 + acc
    o_ref[...] = _rms(x2, gf_ref[...]) if final else x2


def _out_ffn(x, yr, ya, wo, g2, wg, wu, wd, gf, *, tm, final):
    n = x.shape[0]
    row = lambda i: (i, 0)
    const = lambda i: (0, 0)
    resident = functools.partial(pl.BlockSpec, index_map=const, pipeline_mode=pl.Buffered(1))
    return pl.pallas_call(
        functools.partial(_out_ffn_kernel, final=final),
        grid=(n // tm,),
        in_specs=[
            pl.BlockSpec((tm, D_MODEL), row),
            pl.BlockSpec((tm, W_RET_V), row),
            pl.BlockSpec((tm, W_ATT), row),
            resident((W_RET_V + W_ATT, D_MODEL)),
            pl.BlockSpec((1, D_MODEL), const),
            resident((D_MODEL, D_FF)),
            resident((D_MODEL, D_FF)),
            resident((D_FF, D_MODEL)),
            pl.BlockSpec((1, D_MODEL), const),
        ],
        out_specs=pl.BlockSpec((tm, D_MODEL), row),
        out_shape=jax.ShapeDtypeStruct((n, D_MODEL), F32),
        compiler_params=_cparams(1),
        name="out_ffn",
    )(x, yr, ya, wo, g2.reshape(1, D_MODEL), wg, wu, wd, gf.reshape(1, D_MODEL))


def kernel(x_prompt, x_sample, state_ret, cache_k, cache_v, norm1_g, w_in, w_gn, w_o, norm2_g,
           w_gate, w_up, w_down, final_g):
    batch, seq, _ = x_prompt.shape
    dec_batch, dec_seq, _ = x_sample.shape
    depth = w_in.shape[0]
    past = cache_k.shape[2]
    assert dec_seq == 1 and past == WINDOW_MAX
    assert all(seq % (Q_TILE * d) == 0 and seq // (Q_TILE * d) >= 2 for _, d in BRANCHES)
    keep = min(WINDOW_MAX, seq)

    tm = 512
    pos_p = jnp.arange(seq, dtype=F32)
    pos_s = jnp.full((dec_batch,), float(past), F32) + jnp.arange(dec_seq, dtype=F32)[0]
    tabs_p = _ret_tables(pos_p) + _rope_tables(pos_p)
    tabs_s = _ret_tables(pos_s) + _rope_tables(pos_s)

    w_in_b = w_in.astype(BF16)
    w_o_b = w_o.astype(BF16)
    w_gate_b = w_gate.astype(BF16)
    w_up_b = w_up.astype(BF16)
    w_down_b = w_down.astype(BF16)
    kt = jnp.transpose(cache_k, (0, 1, 3, 4, 2)).reshape(depth, dec_batch, W_ATT, past)
    vt = jnp.transpose(cache_v, (0, 1, 3, 4, 2)).reshape(depth, dec_batch, W_ATT, past)

    xp = x_prompt.reshape(batch * seq, D_MODEL)
    xs = x_sample.reshape(dec_batch, D_MODEL)
    st_p, kp_l, vp_l, st_s, ks_l, vs_l = [], [], [], [], [], []
    for l in range(depth):
        final = l == depth - 1
        ffn_w = (w_o_b[l], norm2_g[l], w_gate_b[l], w_up_b[l], w_down_b[l], final_g)

        qr, kr, vr, gr, qa, ka, va = _in_proj(
            xp, norm1_g[l], w_in_b[l], tabs_p, tm=tm, tab_blocks=seq // tm, act_dtype=BF16)
        y_ret, r_p = _ret_prompt(qr, kr, vr, gr, w_gn[l], batch, seq, tt=tm)
        y_att, kc, vc = _att_prompt(qa, ka, va, batch, seq)
        xp = _out_ffn(xp, y_ret, y_att, *ffn_w, tm=tm, final=final)
        st_p.append(r_p.reshape(batch, H_RET, DK_RET, DV_RET))
        kp_l.append(kc.reshape(batch, keep, H_ATT, HD_ATT))
        vp_l.append(vc.reshape(batch, keep, H_ATT, HD_ATT))

        qr, kr, vr, gr, qa, ka, va = _in_proj(
            xs, norm1_g[l], w_in_b[l], tabs_s, tm=dec_batch, tab_blocks=1, act_dtype=F32)
        y_ret, r_s = _ret_sample(
            qr, kr, vr, gr, state_ret[l].reshape(dec_batch, W_RET_QK, DV_RET), w_gn[l], sb=16)
        y_att = _att_sample(qa, ka, va, kt, vt, l, sb=2)
        xs = _out_ffn(xs, y_ret, y_att, *ffn_w, tm=dec_batch, final=final)
        st_s.append(r_s.reshape(dec_batch, H_RET, DK_RET, DV_RET))
        ks_l.append(ka.reshape(dec_batch, dec_seq, H_ATT, HD_ATT))
        vs_l.append(va.reshape(dec_batch, dec_seq, H_ATT, HD_ATT))

    return (
        xp.reshape(batch, seq, D_MODEL),
        xs.reshape(dec_batch, dec_seq, D_MODEL),
        jnp.stack(st_p), jnp.stack(kp_l), jnp.stack(vp_l),
        jnp.stack(st_s), jnp.stack(ks_l), jnp.stack(vs_l),
    )
```

```python
import functools

import jax
import jax.numpy as jnp
import numpy as np
from jax import lax
from jax.experimental import pallas as pl
from jax.experimental.pallas import tpu as pltpu

F32 = jnp.float32
BF16 = jnp.bfloat16

D_MODEL = 1024
H_RET = 4
DK_RET = 64
DV_RET = 128
RET_CHUNK = 128
RET_THETA = 10000.0
H_ATT = 8
HD_ATT = 64
BRANCHES = ((128, 1), (512, 4), (2048, 16))
WINDOW_MAX = 2048
ROPE_THETA = 10000.0
W_RET_QK = H_RET * DK_RET
W_RET_V = H_RET * DV_RET
W_ATT = H_ATT * HD_ATT
IN_WIDTH = 2 * W_RET_QK + 2 * W_RET_V + 3 * W_ATT
D_FF = 2816
NORM_EPS = 1e-6
LOG2E = 1.4426950408889634

LANES = 128
Q_TILE = 128
VMEM_LIMIT = 56 * 1024 * 1024

FF_CHUNKS = ((0, 1024), (1024, 1024), (2048, 768))


def _cparams(n_axes):
    return pltpu.CompilerParams(
        dimension_semantics=("arbitrary",) * n_axes, vmem_limit_bytes=VMEM_LIMIT
    )


def _rms(x, g):
    return (x * lax.rsqrt(jnp.mean(x * x, axis=-1, keepdims=True) + NORM_EPS)) * g


def _silu(x):
    return x * (1.0 / (1.0 + jnp.exp(-x)))


def _dot(a, b):
    return jnp.dot(a, b, preferred_element_type=F32)


def _dot_nt(a, b):
    return lax.dot_general(a, b, (((1,), (1,)), ((), ())), preferred_element_type=F32)


def _dot_tn(a, b):
    return lax.dot_general(a, b, (((0,), (0,)), ((), ())), preferred_element_type=F32)


def _ret_tables(pos):
    angle = 1.0 / (RET_THETA ** jnp.linspace(0.0, 1.0, DK_RET // 2, dtype=F32))
    angle = jnp.repeat(angle, 2)
    ang = pos[:, None] * angle[None, :]
    cos, sin = jnp.cos(ang), jnp.sin(ang)
    even = (jnp.arange(DK_RET) % 2) == 0
    ssin = jnp.where(even[None, :], -sin, sin)
    return jnp.tile(cos, (1, 2)), jnp.tile(ssin, (1, 2))


def _rope_tables(pos):
    half = HD_ATT // 2
    inv = ROPE_THETA ** (-2.0 * jnp.arange(half, dtype=F32) / HD_ATT)
    ang = pos[:, None] * inv[None, :]
    cos, sin = jnp.cos(ang), jnp.sin(ang)
    cosf = jnp.concatenate([cos, cos], axis=-1)
    ssin = jnp.concatenate([-sin, sin], axis=-1)
    return jnp.tile(cosf, (1, 2)), jnp.tile(ssin, (1, 2))


def _decay_tables(C):
    log_gamma = jnp.log1p(-jnp.exp2(-5.0 - jnp.arange(H_RET, dtype=F32)))
    idx = jnp.arange(C, dtype=F32)
    diff = idx[:, None] - idx[None, :]
    dmask = jnp.where(diff >= 0, jnp.exp(jnp.maximum(diff, 0.0) * log_gamma[:, None, None]), 0.0)
    q_decay = jnp.exp((idx + 1.0)[None, :] * log_gamma[:, None])
    k_decay = jnp.exp((C - 1.0 - idx)[None, :] * log_gamma[:, None])
    chunk_decay = jnp.exp(C * log_gamma)
    qdec = jnp.broadcast_to(q_decay[:, :, None], (H_RET, C, DV_RET))
    kdec = jnp.repeat(k_decay.T, DK_RET, axis=1)
    cdec = jnp.broadcast_to(jnp.repeat(chunk_decay, DK_RET)[:, None], (W_RET_QK, DV_RET))
    return dmask, qdec, kdec, cdec


def _in_proj_kernel(x_ref, g_ref, w_ref, rc_ref, rs_ref, ac_ref, as_ref,
                    qr_ref, kr_ref, vr_ref, gr_ref, *rest, prompt):
    xn = _rms(x_ref[...], g_ref[...]).astype(BF16)
    lane = lax.broadcasted_iota(jnp.int32, (1, LANES), 1)
    even = (lane & 1) == 0
    low = (lane & (HD_ATT - 1)) < HD_ATT // 2

    def proj(c0, n):
        return _dot(xn, w_ref[:, c0:c0 + n])

    def rot_pairs(y):
        partner = jnp.where(even, pltpu.roll(y, LANES - 1, 1), pltpu.roll(y, 1, 1))
        return y * rc_ref[...] + partner * rs_ref[...]

    def rot_half(y):
        h = HD_ATT // 2
        partner = jnp.where(low, pltpu.roll(y, LANES - h, 1), pltpu.roll(y, h, 1))
        return y * ac_ref[...] + partner * as_ref[...]

    y = proj(0, 2 * W_RET_QK)
    for c in range(W_RET_QK // LANES):
        sl = slice(c * LANES, (c + 1) * LANES)
        qr_ref[:, sl] = rot_pairs(y[:, sl]).astype(qr_ref.dtype)
        k = rot_pairs(y[:, W_RET_QK + c * LANES:W_RET_QK + (c + 1) * LANES]) * (DK_RET ** -0.5)
        kr_ref[:, sl] = k.astype(kr_ref.dtype)
    c0 = 2 * W_RET_QK
    vr_ref[...] = proj(c0, W_RET_V).astype(vr_ref.dtype)
    c0 += W_RET_V
    gr_ref[...] = proj(c0, W_RET_V).astype(gr_ref.dtype)
    c0 += W_RET_V
    n_slab = W_ATT // LANES
    q_scale = HD_ATT ** -0.5
    if not prompt:
        qa_ref, ka_ref, va_ref = rest
        y = proj(c0, W_ATT)
        for c in range(n_slab):
            sl = slice(c * LANES, (c + 1) * LANES)
            qa_ref[:, sl] = rot_half(y[:, sl]) * q_scale
        c0 += W_ATT
        y = proj(c0, W_ATT)
        for c in range(n_slab):
            sl = slice(c * LANES, (c + 1) * LANES)
            ka_ref[:, sl] = rot_half(y[:, sl])
        c0 += W_ATT
        va_ref[...] = proj(c0, W_ATT)
        return

    att_refs, (kc_ref, vc_ref, ys_ref) = rest[:-3], rest[-3:]
    n_dil = len(BRANCHES)
    tm = x_ref.shape[0]

    def emit(slab_fn, refs, cache_ref):
        for c in range(n_slab):
            sl = slice(c * LANES, (c + 1) * LANES)
            yc = slab_fn(c)
            ys_ref[c] = yc
            if cache_ref is not None:
                cache_ref[sl, :] = yc.T
        for ref, (_, d) in zip(refs, BRANCHES):
            for r in range(d):
                for c in range(n_slab):
                    sl = slice(c * LANES, (c + 1) * LANES)
                    ref[r, :, sl] = ys_ref[c, pl.ds(r, tm // d, stride=d), :].astype(ref.dtype)

    y = proj(c0, W_ATT)
    emit(lambda c: rot_half(y[:, c * LANES:(c + 1) * LANES]) * (q_scale * LOG2E), att_refs[:n_dil], None)
    c0 += W_ATT
    y = proj(c0, W_ATT)
    emit(lambda c: rot_half(y[:, c * LANES:(c + 1) * LANES]), att_refs[n_dil:2 * n_dil], kc_ref)
    c0 += W_ATT
    y = proj(c0, W_ATT)
    emit(lambda c: y[:, c * LANES:(c + 1) * LANES], att_refs[2 * n_dil:], vc_ref)


def _in_proj_specs(tm):
    const = lambda i: (0, 0)
    return [
        pl.BlockSpec((tm, D_MODEL), lambda i: (i, 0)),
        pl.BlockSpec((1, D_MODEL), const),
        pl.BlockSpec((D_MODEL, IN_WIDTH), const),
    ]


def _in_proj_sample(x, g, w, tabs):
    n = x.shape[0]
    row = lambda i: (i, 0)
    widths = [W_RET_QK, W_RET_QK, W_RET_V, W_RET_V, W_ATT, W_ATT, W_ATT]
    return pl.pallas_call(
        functools.partial(_in_proj_kernel, prompt=False),
        grid=(1,),
        in_specs=_in_proj_specs(n) + [pl.BlockSpec((n, LANES), row)] * 4,
        out_specs=[pl.BlockSpec((n, wd), row) for wd in widths],
        out_shape=[jax.ShapeDtypeStruct((n, wd), F32) for wd in widths],
        compiler_params=_cparams(1),
        name="in_proj_sample",
    )(x, g.reshape(1, D_MODEL), w, *tabs)


def _in_proj_prompt(x, g, w, tabs, batch, seq, *, tm):
    n = batch * seq
    nt = seq // tm
    keep = min(WINDOW_MAX, seq)
    row = lambda i: (i, 0)
    tab = lambda i: (i % nt, 0)
    ret_out = [W_RET_QK, W_RET_QK, W_RET_V, W_RET_V]
    out_specs = [pl.BlockSpec((tm, wd), row) for wd in ret_out]
    out_shape = [jax.ShapeDtypeStruct((n, wd), BF16) for wd in ret_out]
    for _ in range(3):
        for _, d in BRANCHES:
            out_specs.append(pl.BlockSpec((None, d, tm // d, W_ATT), lambda i: (i // nt, 0, i % nt, 0)))
            out_shape.append(jax.ShapeDtypeStruct((batch, d, seq // d, W_ATT), BF16))
    first_kept = (seq - keep) // tm
    for _ in range(2):
        out_specs.append(pl.BlockSpec((None, W_ATT, tm),
                                      lambda i: (i // nt, 0, jnp.maximum(i % nt - first_kept, 0))))
        out_shape.append(jax.ShapeDtypeStruct((batch, W_ATT, keep), F32))
    outs = pl.pallas_call(
        functools.partial(_in_proj_kernel, prompt=True),
        grid=(n // tm,),
        in_specs=_in_proj_specs(tm) + [pl.BlockSpec((tm, LANES), tab)] * 4,
        out_specs=out_specs,
        out_shape=out_shape,
        scratch_shapes=[pltpu.VMEM((W_ATT // LANES, tm, LANES), F32)],
        compiler_params=_cparams(1),
        name="in_proj_prompt",
    )(x, g.reshape(1, D_MODEL), w, *tabs)
    att = [o.reshape(n, W_ATT) for o in outs[4:4 + 3 * len(BRANCHES)]]
    return outs[:4], att, outs[-2:]


def _group_norm_gate(o, gate, wgn):
    mu = jnp.mean(o, axis=-1, keepdims=True)
    var = jnp.mean(jnp.square(o - mu), axis=-1, keepdims=True)
    yn = (o - mu) * lax.rsqrt(var + NORM_EPS)
    return _silu(gate) * (yn * wgn)


def _ret_prompt_kernel(q_ref, k_ref, v_ref, g_ref, wgn_ref, dmask_ref, qdec_ref, kdec_ref, cdec_ref,
                       y_ref, st_ref, r_ref, *, n_chunks):
    t = pl.program_id(1)

    @pl.when(t == 0)
    def _():
        r_ref[...] = jnp.zeros_like(r_ref)

    head_of_lane = lax.broadcasted_iota(jnp.int32, (1, W_RET_QK), 1) // DK_RET
    C = RET_CHUNK
    for c in range(n_chunks):
        rows = slice(c * C, (c + 1) * C)
        q = q_ref[rows, :]
        k = k_ref[rows, :]
        v = v_ref[rows, :]
        qm = jnp.concatenate(
            [jnp.where(head_of_lane == h, q, jnp.zeros_like(q)) for h in range(H_RET)], axis=0)
        s_all = _dot_nt(qm, k)
        x_all = _dot(qm, r_ref[...].astype(BF16))
        for h in range(H_RET):
            hr = slice(h * C, (h + 1) * C)
            hv = slice(h * DV_RET, (h + 1) * DV_RET)
            s = s_all[hr] * dmask_ref[h]
            o = _dot(s.astype(BF16), v[:, hv]) + x_all[hr] * qdec_ref[h]
            y = _group_norm_gate(o, g_ref[rows, hv].astype(F32), wgn_ref[:, hv])
            y_ref[rows, hv] = y.astype(y_ref.dtype)
        kd = (k.astype(F32) * kdec_ref[...]).astype(BF16)
        kv = _dot_tn(kd, v)
        for h in range(H_RET):
            hk = slice(h * DK_RET, (h + 1) * DK_RET)
            hv = slice(h * DV_RET, (h + 1) * DV_RET)
            r_ref[hk, :] = r_ref[hk, :] * cdec_ref[hk, :] + kv[hk, hv]

    @pl.when(t == pl.num_programs(1) - 1)
    def _():
        st_ref[...] = r_ref[...]


def _ret_prompt(qr, kr, vr, gr, wgn, batch, seq, *, tt):
    dmask, qdec, kdec, cdec = _decay_tables(RET_CHUNK)
    nt = seq // tt
    row = lambda b, t: (b * nt + t, 0)
    c2 = lambda b, t: (0, 0)
    c3 = lambda b, t: (0, 0, 0)
    return pl.pallas_call(
        functools.partial(_ret_prompt_kernel, n_chunks=tt // RET_CHUNK),
        grid=(batch, nt),
        in_specs=[
            pl.BlockSpec((tt, W_RET_QK), row),
            pl.BlockSpec((tt, W_RET_QK), row),
            pl.BlockSpec((tt, W_RET_V), row),
            pl.BlockSpec((tt, W_RET_V), row),
            pl.BlockSpec((1, W_RET_V), c2),
            pl.BlockSpec((H_RET, RET_CHUNK, RET_CHUNK), c3),
            pl.BlockSpec((H_RET, RET_CHUNK, DV_RET), c3),
            pl.BlockSpec((RET_CHUNK, W_RET_QK), c2),
            pl.BlockSpec((W_RET_QK, DV_RET), c2),
        ],
        out_specs=[
            pl.BlockSpec((tt, W_RET_V), row),
            pl.BlockSpec((None, W_RET_QK, DV_RET), lambda b, t: (b, 0, 0)),
        ],
        out_shape=[
            jax.ShapeDtypeStruct((batch * seq, W_RET_V), BF16),
            jax.ShapeDtypeStruct((batch, W_RET_QK, DV_RET), F32),
        ],
        scratch_shapes=[pltpu.VMEM((W_RET_QK, DV_RET), F32)],
        compiler_params=_cparams(2),
        name="ret_prompt",
    )(qr, kr, vr, gr, wgn.reshape(1, W_RET_V), dmask, qdec, kdec, cdec)


def _ret_sample_kernel(q_ref, k_ref, v_ref, g_ref, st_ref, wgn_ref, dmask_ref, qdec_ref, kdec_ref, cdec_ref,
                       y_ref, so_ref, *, sb):
    head_row = lax.broadcasted_iota(jnp.int32, (H_RET, W_RET_QK), 0)
    head_lane = lax.broadcasted_iota(jnp.int32, (H_RET, W_RET_QK), 1) // DK_RET
    own = head_row == head_lane
    for s in range(sb):
        qm = jnp.where(own, q_ref[s:s + 1, :], 0.0)
        km = jnp.where(own, k_ref[s:s + 1, :], 0.0)
        r = st_ref[s]
        x = _dot(qm.astype(BF16), r.astype(BF16))
        sc = jnp.sum(qm * km, axis=-1, keepdims=True) * dmask_ref[...]
        v4 = v_ref[s]
        o = sc * v4 + x * qdec_ref[...]
        y_ref[s] = _group_norm_gate(o, g_ref[s], wgn_ref[...])
        kv = _dot_tn((km * kdec_ref[...]).astype(BF16), v4.astype(BF16))
        so_ref[s] = r * cdec_ref[...] + kv


def _ret_sample(qr, kr, vr, gr, state, wgn, *, sb):
    b = qr.shape[0]
    dmask, qdec, kdec, cdec = _decay_tables(1)
    row = lambda i: (i, 0)
    row3 = lambda i: (i, 0, 0)
    c2 = lambda i: (0, 0)
    y, so = pl.pallas_call(
        functools.partial(_ret_sample_kernel, sb=sb),
        grid=(b // sb,),
        in_specs=[
            pl.BlockSpec((sb, W_RET_QK), row),
            pl.BlockSpec((sb, W_RET_QK), row),
            pl.BlockSpec((sb, H_RET, DV_RET), row3),
            pl.BlockSpec((sb, H_RET, DV_RET), row3),
            pl.BlockSpec((sb, W_RET_QK, DV_RET), row3),
            pl.BlockSpec((H_RET, DV_RET), c2),
            pl.BlockSpec((H_RET, 1), c2),
            pl.BlockSpec((H_RET, DV_RET), c2),
            pl.BlockSpec((1, W_RET_QK), c2),
            pl.BlockSpec((W_RET_QK, DV_RET), c2),
        ],
        out_specs=[
            pl.BlockSpec((sb, H_RET, DV_RET), row3),
            pl.BlockSpec((sb, W_RET_QK, DV_RET), row3),
        ],
        out_shape=[
            jax.ShapeDtypeStruct((b, H_RET, DV_RET), F32),
            jax.ShapeDtypeStruct((b, W_RET_QK, DV_RET), F32),
        ],
        compiler_params=_cparams(1),
        name="ret_sample",
    )(qr, kr, vr.reshape(b, H_RET, DV_RET), gr.reshape(b, H_RET, DV_RET), state,
      wgn.reshape(H_RET, DV_RET), dmask.reshape(H_RET, 1), qdec.reshape(H_RET, DV_RET), kdec, cdec)
    return y.reshape(b, W_RET_V), so


def _att_prompt_kernel(*refs, seq, unroll):
    T = Q_TILE
    n_br = len(BRANCHES)
    q_refs, k_refs, v_refs = refs[:n_br], refs[n_br:2 * n_br], refs[2 * n_br:3 * n_br]
    bias_ref, o_ref = refs[3 * n_br], refs[3 * n_br + 1]
    part_refs, tmp_ref = refs[3 * n_br + 2:-1], refs[-1]
    head0 = lax.broadcasted_iota(jnp.int32, (1, LANES), 1) < HD_ATT
    n_tiles = seq // T

    def tile(bi, it):
        nblk = seq // (T * BRANCHES[bi][1])
        first = lax.rem(it, nblk) == 0
        qs = pl.multiple_of(it * T, T)
        ks = pl.multiple_of(jnp.where(first, it, it - 1) * T, T)
        bias = bias_ref[jnp.where(first, 0, 1)]
        q = q_refs[bi][pl.ds(qs, T), :]
        kk = k_refs[bi][pl.ds(ks, 2 * T), :]
        vv = v_refs[bi][pl.ds(ks, 2 * T), :]
        zero = jnp.zeros_like(q)
        q2 = jnp.concatenate([jnp.where(head0, q, zero), jnp.where(head0, zero, q)], axis=0)
        s = _dot_nt(q2, kk)
        ps, ms, ls = [], [], []
        for h in range(2):
            sh = s[h * T:(h + 1) * T] + bias
            mh = jnp.max(sh, axis=-1, keepdims=True)
            ph = jnp.exp2(sh - mh)
            ps.append(ph.astype(BF16))
            ms.append(mh)
            ls.append(jnp.sum(ph, axis=-1, keepdims=True))
        pv = _dot(jnp.concatenate(ps, axis=0), vv)
        l = jnp.where(head0, ls[0], ls[1])
        out = jnp.where(head0, pv[:T], pv[T:]) / l
        lse = jnp.where(head0, ms[0], ms[1]) + jnp.log2(l)
        return qs, out, lse

    for bi in range(n_br - 1, 0, -1):
        out_ref, lse_ref = part_refs[2 * (bi - 1)], part_refs[2 * (bi - 1) + 1]

        def body(g, carry, bi=bi, out_ref=out_ref, lse_ref=lse_ref):
            for u in range(unroll):
                qs, out, lse = tile(bi, g * unroll + u)
                out_ref[pl.ds(qs, T), :] = out
                lse_ref[pl.ds(qs, T), :] = lse
            return carry

        lax.fori_loop(0, n_tiles // unroll, body, 0)

    def body1(g, carry):
        for u in range(unroll):
            it = g * unroll + u
            qs, out, lse = tile(0, it)
            outs, lses = [out], [lse]
            for bi in range(1, n_br):
                d = BRANCHES[bi][1]
                per = T // d
                for j, src_ref in enumerate(part_refs[2 * (bi - 1):2 * bi]):
                    slot = 2 * (bi - 1) + j
                    for r in range(d):
                        src = pl.ds(pl.multiple_of(r * (seq // d) + it * per, per), per)
                        tmp_ref[u, slot, pl.ds(r, per, stride=d), :] = src_ref[src, :]
                outs.append(tmp_ref[u, 2 * (bi - 1)])
                lses.append(tmp_ref[u, 2 * (bi - 1) + 1])
            top = functools.reduce(jnp.maximum, lses)
            wts = [jnp.exp2(e - top) for e in lses]
            num = functools.reduce(lambda a, b: a + b, [w * o for w, o in zip(wts, outs)])
            den = functools.reduce(lambda a, b: a + b, wts)
            o_ref[pl.ds(qs, T), :] = (num / den).astype(o_ref.dtype)
        return carry

    lax.fori_loop(0, n_tiles // unroll, body1, 0)


def _att_bias():
    T = Q_TILE
    i = np.arange(T)[:, None]
    c = np.arange(2 * T)[None, :]
    dist = np.stack([i - c, i - c + T])
    return jnp.asarray(np.where((dist >= 0) & (dist <= T), 0.0, -np.inf).astype(np.float32))


def _att_prompt(att, batch, seq, *, unroll):
    assert BRANCHES[0][1] == 1 and all(w // d == Q_TILE for w, d in BRANCHES)
    T = Q_TILE
    nhp = W_ATT // LANES
    n_dil = len(BRANCHES) - 1
    blk = lambda b, h: (b, h)
    return pl.pallas_call(
        functools.partial(_att_prompt_kernel, seq=seq, unroll=unroll),
        grid=(batch, nhp),
        in_specs=[pl.BlockSpec((seq, LANES), blk)] * len(att)
        + [pl.BlockSpec((2, T, 2 * T), lambda b, h: (0, 0, 0))],
        out_specs=pl.BlockSpec((seq, LANES), blk),
        out_shape=jax.ShapeDtypeStruct((batch * seq, W_ATT), BF16),
        scratch_shapes=[pltpu.VMEM((seq, LANES), F32)] * (2 * n_dil)
        + [pltpu.VMEM((unroll, 2 * n_dil, T, LANES), F32)],
        compiler_params=_cparams(2),
        name="att_prompt",
    )(*att, _att_bias())


def _att_sample_kernel(q_ref, kn_ref, vn_ref, cnt_ref, kt_ref, vt_ref, o_ref, *, sb):
    i = pl.program_id(0)
    head_row = lax.broadcasted_iota(jnp.int32, (H_ATT, W_ATT), 0)
    head_lane = lax.broadcasted_iota(jnp.int32, (H_ATT, W_ATT), 1) // HD_ATT
    own = head_row == head_lane
    cnt = cnt_ref[...]
    live = cnt > 0.0
    n_br = float(len(BRANCHES))
    for s in range(sb):
        row = pl.ds(i * sb + s, 1)
        qm = jnp.where(own, q_ref[row, :], 0.0)
        s_new = jnp.sum(qm * kn_ref[row, :], axis=-1, keepdims=True)
        sc = _dot(qm.astype(BF16), kt_ref[s].astype(BF16))
        sc = jnp.where(live, sc, -jnp.inf)
        m = jnp.maximum(jnp.max(sc, axis=-1, keepdims=True), s_new)
        p = cnt * jnp.exp(sc - m)
        p_new = n_br * jnp.exp(s_new - m)
        den = jnp.sum(p, axis=-1, keepdims=True) + p_new
        acc = _dot_nt(p.astype(BF16), vt_ref[s].astype(BF16)) + p_new * vn_ref[row, :]
        out = jnp.where(own, acc / den, 0.0)
        o_ref[row, :] = jnp.sum(out, axis=0, keepdims=True)


def _att_sample(qa, ka, va, kt, vt, layer, *, sb):
    _, b, _, past = kt.shape
    dist = past - np.arange(past)
    cnt = sum(((dist % d == 0) & (dist <= w)).astype(np.float32) for w, d in BRANCHES)
    cnt = jnp.asarray(cnt.reshape(1, past))
    whole = pl.BlockSpec((b, W_ATT), lambda i: (0, 0))
    cache = pl.BlockSpec((None, sb, W_ATT, past), lambda i: (layer, i, 0, 0))
    return pl.pallas_call(
        functools.partial(_att_sample_kernel, sb=sb),
        grid=(b // sb,),
        in_specs=[whole, whole, whole, pl.BlockSpec((1, past), lambda i: (0, 0)), cache, cache],
        out_specs=whole,
        out_shape=jax.ShapeDtypeStruct((b, W_ATT), F32),
        compiler_params=_cparams(1),
        name="att_sample",
    )(qa, ka, va, cnt, kt, vt)


def _out_ffn_kernel(x_ref, yr_ref, ya_ref, wo_ref, g2_ref, wg_ref, wu_ref, wd_ref, gf_ref, o_ref, *, final):
    mix = (_dot(yr_ref[...].astype(BF16), wo_ref[:W_RET_V, :])
           + _dot(ya_ref[...].astype(BF16), wo_ref[W_RET_V:, :]))
    x1 = x_ref[...] + mix
    xn = _rms(x1, g2_ref[...]).astype(BF16)
    acc = None
    for c0, n in FF_CHUNKS:
        gate = _dot(xn, wg_ref[:, c0:c0 + n])
        up = _dot(xn, wu_ref[:, c0:c0 + n])
        part = _dot((_silu(gate) * up).astype(BF16), wd_ref[c0:c0 + n, :])
        acc = part if acc is None else acc + part
    x2 = x1 + acc
    o_ref[...] = _rms(x2, gf_ref[...]) if final else x2


def _out_ffn(x, yr, ya, wo, g2, wg, wu, wd, gf, *, tm, final):
    n = x.shape[0]
    row = lambda i: (i, 0)
    const = lambda i: (0, 0)
    resident = functools.partial(pl.BlockSpec, index_map=const, pipeline_mode=pl.Buffered(1))
    return pl.pallas_call(
        functools.partial(_out_ffn_kernel, final=final),
        grid=(n // tm,),
        in_specs=[
            pl.BlockSpec((tm, D_MODEL), row),
            pl.BlockSpec((tm, W_RET_V), row),
            pl.BlockSpec((tm, W_ATT), row),
            resident((W_RET_V + W_ATT, D_MODEL)),
            pl.BlockSpec((1, D_MODEL), const),
            resident((D_MODEL, D_FF)),
            resident((D_MODEL, D_FF)),
            resident((D_FF, D_MODEL)),
            pl.BlockSpec((1, D_MODEL), const),
        ],
        out_specs=pl.BlockSpec((tm, D_MODEL), row),
        out_shape=jax.ShapeDtypeStruct((n, D_MODEL), F32),
        compiler_params=_cparams(1),
        name="out_ffn",
    )(x, yr, ya, wo, g2.reshape(1, D_MODEL), wg, wu, wd, gf.reshape(1, D_MODEL))


def kernel(x_prompt, x_sample, state_ret, cache_k, cache_v, norm1_g, w_in, w_gn, w_o, norm2_g,
           w_gate, w_up, w_down, final_g):
    batch, seq, _ = x_prompt.shape
    dec_batch, dec_seq, _ = x_sample.shape
    depth = w_in.shape[0]
    past = cache_k.shape[2]
    assert dec_seq == 1 and past == WINDOW_MAX
    assert all(seq % (Q_TILE * d) == 0 and seq // (Q_TILE * d) >= 2 for _, d in BRANCHES)
    keep = min(WINDOW_MAX, seq)

    tm = 512
    pos_p = jnp.arange(seq, dtype=F32)
    pos_s = jnp.full((dec_batch,), float(past), F32) + jnp.arange(dec_seq, dtype=F32)[0]
    tabs_p = _ret_tables(pos_p) + _rope_tables(pos_p)
    tabs_s = _ret_tables(pos_s) + _rope_tables(pos_s)

    w_in_b = w_in.astype(BF16)
    w_o_b = w_o.astype(BF16)
    w_gate_b = w_gate.astype(BF16)
    w_up_b = w_up.astype(BF16)
    w_down_b = w_down.astype(BF16)
    kt = jnp.transpose(cache_k, (0, 1, 3, 4, 2)).reshape(depth, dec_batch, W_ATT, past)
    vt = jnp.transpose(cache_v, (0, 1, 3, 4, 2)).reshape(depth, dec_batch, W_ATT, past)

    xp = x_prompt.reshape(batch * seq, D_MODEL)
    xs = x_sample.reshape(dec_batch, D_MODEL)
    st_p, kp_l, vp_l, st_s, ks_l, vs_l = [], [], [], [], [], []
    for l in range(depth):
        final = l == depth - 1
        ffn_w = (w_o_b[l], norm2_g[l], w_gate_b[l], w_up_b[l], w_down_b[l], final_g)

        (qr, kr, vr, gr), att, (kc, vc) = _in_proj_prompt(
            xp, norm1_g[l], w_in_b[l], tabs_p, batch, seq, tm=tm)
        y_ret, r_p = _ret_prompt(qr, kr, vr, gr, w_gn[l], batch, seq, tt=tm)
        y_att = _att_prompt(att, batch, seq, unroll=8)
        xp = _out_ffn(xp, y_ret, y_att, *ffn_w, tm=tm, final=final)
        st_p.append(r_p.reshape(batch, H_RET, DK_RET, DV_RET))
        kp_l.append(jnp.transpose(kc.reshape(batch, H_ATT, HD_ATT, keep), (0, 3, 1, 2)))
        vp_l.append(jnp.transpose(vc.reshape(batch, H_ATT, HD_ATT, keep), (0, 3, 1, 2)))

        qr, kr, vr, gr, qa, ka, va = _in_proj_sample(xs, norm1_g[l], w_in_b[l], tabs_s)
        y_ret, r_s = _ret_sample(
            qr, kr, vr, gr, state_ret[l].reshape(dec_batch, W_RET_QK, DV_RET), w_gn[l], sb=16)
        y_att = _att_sample(qa, ka, va, kt, vt, l, sb=2)
        xs = _out_ffn(xs, y_ret, y_att, *ffn_w, tm=dec_batch, final=final)
        st_s.append(r_s.reshape(dec_batch, H_RET, DK_RET, DV_RET))
        ks_l.append(ka.reshape(dec_batch, dec_seq, H_ATT, HD_ATT))
        vs_l.append(va.reshape(dec_batch, dec_seq, H_ATT, HD_ATT))

    return (
        xp.reshape(batch, seq, D_MODEL),
        xs.reshape(dec_batch, dec_seq, D_MODEL),
        jnp.stack(st_p), jnp.stack(kp_l), jnp.stack(vp_l),
        jnp.stack(st_s), jnp.stack(ks_l), jnp.stack(vs_l),
    )
```

```python
import functools

import jax
import jax.numpy as jnp
import numpy as np
from jax import lax
from jax.experimental import pallas as pl
from jax.experimental.pallas import tpu as pltpu

F32 = jnp.float32
BF16 = jnp.bfloat16

D_MODEL = 1024
H_RET = 4
DK_RET = 64
DV_RET = 128
RET_CHUNK = 128
RET_THETA = 10000.0
H_ATT = 8
HD_ATT = 64
BRANCHES = ((128, 1), (512, 4), (2048, 16))
WINDOW_MAX = 2048
ROPE_THETA = 10000.0
W_RET_QK = H_RET * DK_RET
W_RET_V = H_RET * DV_RET
W_ATT = H_ATT * HD_ATT
IN_WIDTH = 2 * W_RET_QK + 2 * W_RET_V + 3 * W_ATT
D_FF = 2816
NORM_EPS = 1e-6
LOG2E = 1.4426950408889634

LANES = 128
Q_TILE = 128
VMEM_LIMIT = 56 * 1024 * 1024

FF_CHUNKS = ((0, 512), (512, 768), (1280, 768), (2048, 768))


def _cparams(n_axes):
    return pltpu.CompilerParams(
        dimension_semantics=("arbitrary",) * n_axes, vmem_limit_bytes=VMEM_LIMIT
    )


def _rms(x, g):
    return (x * lax.rsqrt(jnp.mean(x * x, axis=-1, keepdims=True) + NORM_EPS)) * g


def _silu(x):
    return x * (1.0 / (1.0 + jnp.exp(-x)))


def _dot(a, b):
    return jnp.dot(a, b, preferred_element_type=F32)


def _dot_nt(a, b):
    return lax.dot_general(a, b, (((1,), (1,)), ((), ())), preferred_element_type=F32)


def _dot_tn(a, b):
    return lax.dot_general(a, b, (((0,), (0,)), ((), ())), preferred_element_type=F32)


def _ret_tables(pos):
    angle = 1.0 / (RET_THETA ** jnp.linspace(0.0, 1.0, DK_RET // 2, dtype=F32))
    angle = jnp.repeat(angle, 2)
    ang = pos[:, None] * angle[None, :]
    cos, sin = jnp.cos(ang), jnp.sin(ang)
    even = (jnp.arange(DK_RET) % 2) == 0
    ssin = jnp.where(even[None, :], -sin, sin)
    return jnp.tile(cos, (1, 2)), jnp.tile(ssin, (1, 2))


def _rope_tables(pos):
    half = HD_ATT // 2
    inv = ROPE_THETA ** (-2.0 * jnp.arange(half, dtype=F32) / HD_ATT)
    ang = pos[:, None] * inv[None, :]
    cos, sin = jnp.cos(ang), jnp.sin(ang)
    cosf = jnp.concatenate([cos, cos], axis=-1)
    ssin = jnp.concatenate([-sin, sin], axis=-1)
    return jnp.tile(cosf, (1, 2)), jnp.tile(ssin, (1, 2))


def _decay_tables(C):
    log_gamma = jnp.log1p(-jnp.exp2(-5.0 - jnp.arange(H_RET, dtype=F32)))
    idx = jnp.arange(C, dtype=F32)
    diff = idx[:, None] - idx[None, :]
    dmask = jnp.where(diff >= 0, jnp.exp(jnp.maximum(diff, 0.0) * log_gamma[:, None, None]), 0.0)
    q_decay = jnp.exp((idx + 1.0)[None, :] * log_gamma[:, None])
    k_decay = jnp.exp((C - 1.0 - idx)[None, :] * log_gamma[:, None])
    chunk_decay = jnp.exp(C * log_gamma)
    qdec = jnp.broadcast_to(q_decay[:, :, None], (H_RET, C, DV_RET))
    kdec = jnp.repeat(k_decay.T, DK_RET, axis=1)
    cdec = jnp.broadcast_to(jnp.repeat(chunk_decay, DK_RET)[:, None], (W_RET_QK, DV_RET))
    return dmask, qdec, kdec, cdec


def _in_proj_kernel(x_ref, g_ref, w_ref, rc_ref, rs_ref, ac_ref, as_ref,
                    qr_ref, kr_ref, vr_ref, gr_ref, *rest, prompt):
    xn = _rms(x_ref[...], g_ref[...]).astype(BF16)
    lane = lax.broadcasted_iota(jnp.int32, (1, LANES), 1)
    even = (lane & 1) == 0
    low = (lane & (HD_ATT - 1)) < HD_ATT // 2

    def proj(c0, n):
        return _dot(xn, w_ref[:, c0:c0 + n])

    def rot_pairs(y):
        partner = jnp.where(even, pltpu.roll(y, LANES - 1, 1), pltpu.roll(y, 1, 1))
        return y * rc_ref[...] + partner * rs_ref[...]

    def rot_half(y):
        h = HD_ATT // 2
        partner = jnp.where(low, pltpu.roll(y, LANES - h, 1), pltpu.roll(y, h, 1))
        return y * ac_ref[...] + partner * as_ref[...]

    y = proj(0, 2 * W_RET_QK)
    for c in range(W_RET_QK // LANES):
        sl = slice(c * LANES, (c + 1) * LANES)
        qr_ref[:, sl] = rot_pairs(y[:, sl]).astype(qr_ref.dtype)
        k = rot_pairs(y[:, W_RET_QK + c * LANES:W_RET_QK + (c + 1) * LANES]) * (DK_RET ** -0.5)
        kr_ref[:, sl] = k.astype(kr_ref.dtype)
    c0 = 2 * W_RET_QK
    vr_ref[...] = proj(c0, W_RET_V).astype(vr_ref.dtype)
    c0 += W_RET_V
    gr_ref[...] = proj(c0, W_RET_V).astype(gr_ref.dtype)
    c0 += W_RET_V
    n_slab = W_ATT // LANES
    q_scale = HD_ATT ** -0.5
    if not prompt:
        qa_ref, ka_ref, va_ref = rest
        y = proj(c0, W_ATT)
        for c in range(n_slab):
            sl = slice(c * LANES, (c + 1) * LANES)
            qa_ref[:, sl] = rot_half(y[:, sl]) * q_scale
        c0 += W_ATT
        y = proj(c0, W_ATT)
        for c in range(n_slab):
            sl = slice(c * LANES, (c + 1) * LANES)
            ka_ref[:, sl] = rot_half(y[:, sl])
        c0 += W_ATT
        va_ref[...] = proj(c0, W_ATT)
        return

    att_refs, (kc_ref, vc_ref, ys_ref) = rest[:-3], rest[-3:]
    n_dil = len(BRANCHES)
    tm = x_ref.shape[0]

    def emit(slab_fn, refs, cache_ref):
        for c in range(n_slab):
            sl = slice(c * LANES, (c + 1) * LANES)
            yc = slab_fn(c)
            ys_ref[c] = yc
            if cache_ref is not None:
                cache_ref[sl, :] = yc.T
        for ref, (_, d) in zip(refs, BRANCHES):
            for r in range(d):
                for c in range(n_slab):
                    sl = slice(c * LANES, (c + 1) * LANES)
                    ref[r, :, sl] = ys_ref[c, pl.ds(r, tm // d, stride=d), :].astype(ref.dtype)

    y = proj(c0, W_ATT)
    emit(lambda c: rot_half(y[:, c * LANES:(c + 1) * LANES]) * (q_scale * LOG2E), att_refs[:n_dil], None)
    c0 += W_ATT
    y = proj(c0, W_ATT)
    emit(lambda c: rot_half(y[:, c * LANES:(c + 1) * LANES]), att_refs[n_dil:2 * n_dil], kc_ref)
    c0 += W_ATT
    y = proj(c0, W_ATT)
    emit(lambda c: y[:, c * LANES:(c + 1) * LANES], att_refs[2 * n_dil:], vc_ref)


def _in_proj_specs(tm):
    const = lambda i: (0, 0)
    return [
        pl.BlockSpec((tm, D_MODEL), lambda i: (i, 0)),
        pl.BlockSpec((1, D_MODEL), const),
        pl.BlockSpec((D_MODEL, IN_WIDTH), const),
    ]


def _in_proj_sample(x, g, w, tabs):
    n = x.shape[0]
    row = lambda i: (i, 0)
    widths = [W_RET_QK, W_RET_QK, W_RET_V, W_RET_V, W_ATT, W_ATT, W_ATT]
    return pl.pallas_call(
        functools.partial(_in_proj_kernel, prompt=False),
        grid=(1,),
        in_specs=_in_proj_specs(n) + [pl.BlockSpec((n, LANES), row)] * 4,
        out_specs=[pl.BlockSpec((n, wd), row) for wd in widths],
        out_shape=[jax.ShapeDtypeStruct((n, wd), F32) for wd in widths],
        compiler_params=_cparams(1),
        name="in_proj_sample",
    )(x, g.reshape(1, D_MODEL), w, *tabs)


def _in_proj_prompt(x, g, w, tabs, batch, seq, *, tm):
    n = batch * seq
    nt = seq // tm
    keep = min(WINDOW_MAX, seq)
    row = lambda i: (i, 0)
    tab = lambda i: (i % nt, 0)
    ret_out = [W_RET_QK, W_RET_QK, W_RET_V, W_RET_V]
    out_specs = [pl.BlockSpec((tm, wd), row) for wd in ret_out]
    out_shape = [jax.ShapeDtypeStruct((n, wd), BF16) for wd in ret_out]
    for _ in range(3):
        for _, d in BRANCHES:
            out_specs.append(pl.BlockSpec((None, d, tm // d, W_ATT), lambda i: (i // nt, 0, i % nt, 0)))
            out_shape.append(jax.ShapeDtypeStruct((batch, d, seq // d, W_ATT), BF16))
    first_kept = (seq - keep) // tm
    for _ in range(2):
        out_specs.append(pl.BlockSpec((None, W_ATT, tm),
                                      lambda i: (i // nt, 0, jnp.maximum(i % nt - first_kept, 0))))
        out_shape.append(jax.ShapeDtypeStruct((batch, W_ATT, keep), F32))
    outs = pl.pallas_call(
        functools.partial(_in_proj_kernel, prompt=True),
        grid=(n // tm,),
        in_specs=_in_proj_specs(tm) + [pl.BlockSpec((tm, LANES), tab)] * 4,
        out_specs=out_specs,
        out_shape=out_shape,
        scratch_shapes=[pltpu.VMEM((W_ATT // LANES, tm, LANES), F32)],
        compiler_params=_cparams(1),
        name="in_proj_prompt",
    )(x, g.reshape(1, D_MODEL), w, *tabs)
    att = [o.reshape(n, W_ATT) for o in outs[4:4 + 3 * len(BRANCHES)]]
    return outs[:4], att, outs[-2:]


def _group_norm_gate(o, gate, wgn):
    mu = jnp.mean(o, axis=-1, keepdims=True)
    var = jnp.mean(jnp.square(o - mu), axis=-1, keepdims=True)
    yn = (o - mu) * lax.rsqrt(var + NORM_EPS)
    return _silu(gate) * (yn * wgn)


def _ret_prompt_kernel(q_ref, k_ref, v_ref, g_ref, wgn_ref, dmask_ref, qdec_ref, kdec_ref, cdec_ref,
                       y_ref, st_ref, r_ref, *, n_chunks):
    t = pl.program_id(1)

    @pl.when(t == 0)
    def _():
        r_ref[...] = jnp.zeros_like(r_ref)

    head_of_lane = lax.broadcasted_iota(jnp.int32, (1, W_RET_QK), 1) // DK_RET
    C = RET_CHUNK
    for c in range(n_chunks):
        rows = slice(c * C, (c + 1) * C)
        q = q_ref[rows, :]
        k = k_ref[rows, :]
        v = v_ref[rows, :]
        qm = jnp.concatenate(
            [jnp.where(head_of_lane == h, q, jnp.zeros_like(q)) for h in range(H_RET)], axis=0)
        s_all = _dot_nt(qm, k)
        x_all = _dot(qm, r_ref[...].astype(BF16))
        for h in range(H_RET):
            hr = slice(h * C, (h + 1) * C)
            hv = slice(h * DV_RET, (h + 1) * DV_RET)
            s = s_all[hr] * dmask_ref[h]
            o = _dot(s.astype(BF16), v[:, hv]) + x_all[hr] * qdec_ref[h]
            y = _group_norm_gate(o, g_ref[rows, hv].astype(F32), wgn_ref[:, hv])
            y_ref[rows, hv] = y.astype(y_ref.dtype)
        kd = (k.astype(F32) * kdec_ref[...]).astype(BF16)
        kv = _dot_tn(kd, v)
        for h in range(H_RET):
            hk = slice(h * DK_RET, (h + 1) * DK_RET)
            hv = slice(h * DV_RET, (h + 1) * DV_RET)
            r_ref[hk, :] = r_ref[hk, :] * cdec_ref[hk, :] + kv[hk, hv]

    @pl.when(t == pl.num_programs(1) - 1)
    def _():
        st_ref[...] = r_ref[...]


def _ret_prompt(qr, kr, vr, gr, wgn, batch, seq, *, tt):
    dmask, qdec, kdec, cdec = _decay_tables(RET_CHUNK)
    nt = seq // tt
    row = lambda b, t: (b * nt + t, 0)
    c2 = lambda b, t: (0, 0)
    c3 = lambda b, t: (0, 0, 0)
    return pl.pallas_call(
        functools.partial(_ret_prompt_kernel, n_chunks=tt // RET_CHUNK),
        grid=(batch, nt),
        in_specs=[
            pl.BlockSpec((tt, W_RET_QK), row),
            pl.BlockSpec((tt, W_RET_QK), row),
            pl.BlockSpec((tt, W_RET_V), row),
            pl.BlockSpec((tt, W_RET_V), row),
            pl.BlockSpec((1, W_RET_V), c2),
            pl.BlockSpec((H_RET, RET_CHUNK, RET_CHUNK), c3),
            pl.BlockSpec((H_RET, RET_CHUNK, DV_RET), c3),
            pl.BlockSpec((RET_CHUNK, W_RET_QK), c2),
            pl.BlockSpec((W_RET_QK, DV_RET), c2),
        ],
        out_specs=[
            pl.BlockSpec((tt, W_RET_V), row),
            pl.BlockSpec((None, W_RET_QK, DV_RET), lambda b, t: (b, 0, 0)),
        ],
        out_shape=[
            jax.ShapeDtypeStruct((batch * seq, W_RET_V), BF16),
            jax.ShapeDtypeStruct((batch, W_RET_QK, DV_RET), F32),
        ],
        scratch_shapes=[pltpu.VMEM((W_RET_QK, DV_RET), F32)],
        compiler_params=_cparams(2),
        name="ret_prompt",
    )(qr, kr, vr, gr, wgn.reshape(1, W_RET_V), dmask, qdec, kdec, cdec)


def _ret_sample_kernel(q_ref, k_ref, v_ref, g_ref, st_ref, wgn_ref, dmask_ref, qdec_ref, kdec_ref, cdec_ref,
                       y_ref, so_ref, *, sb):
    head_row = lax.broadcasted_iota(jnp.int32, (H_RET, W_RET_QK), 0)
    head_lane = lax.broadcasted_iota(jnp.int32, (H_RET, W_RET_QK), 1) // DK_RET
    own = head_row == head_lane
    for s in range(sb):
        qm = jnp.where(own, q_ref[s:s + 1, :], 0.0)
        km = jnp.where(own, k_ref[s:s + 1, :], 0.0)
        r = st_ref[s]
        x = _dot(qm.astype(BF16), r.astype(BF16))
        sc = jnp.sum(qm * km, axis=-1, keepdims=True) * dmask_ref[...]
        v4 = v_ref[s]
        o = sc * v4 + x * qdec_ref[...]
        y_ref[s] = _group_norm_gate(o, g_ref[s], wgn_ref[...])
        kv = _dot_tn((km * kdec_ref[...]).astype(BF16), v4.astype(BF16))
        so_ref[s] = r * cdec_ref[...] + kv


def _ret_sample(qr, kr, vr, gr, state, wgn, *, sb):
    b = qr.shape[0]
    dmask, qdec, kdec, cdec = _decay_tables(1)
    row = lambda i: (i, 0)
    row3 = lambda i: (i, 0, 0)
    c2 = lambda i: (0, 0)
    y, so = pl.pallas_call(
        functools.partial(_ret_sample_kernel, sb=sb),
        grid=(b // sb,),
        in_specs=[
            pl.BlockSpec((sb, W_RET_QK), row),
            pl.BlockSpec((sb, W_RET_QK), row),
            pl.BlockSpec((sb, H_RET, DV_RET), row3),
            pl.BlockSpec((sb, H_RET, DV_RET), row3),
            pl.BlockSpec((sb, W_RET_QK, DV_RET), row3),
            pl.BlockSpec((H_RET, DV_RET), c2),
            pl.BlockSpec((H_RET, 1), c2),
            pl.BlockSpec((H_RET, DV_RET), c2),
            pl.BlockSpec((1, W_RET_QK), c2),
            pl.BlockSpec((W_RET_QK, DV_RET), c2),
        ],
        out_specs=[
            pl.BlockSpec((sb, H_RET, DV_RET), row3),
            pl.BlockSpec((sb, W_RET_QK, DV_RET), row3),
        ],
        out_shape=[
            jax.ShapeDtypeStruct((b, H_RET, DV_RET), F32),
            jax.ShapeDtypeStruct((b, W_RET_QK, DV_RET), F32),
        ],
        compiler_params=_cparams(1),
        name="ret_sample",
    )(qr, kr, vr.reshape(b, H_RET, DV_RET), gr.reshape(b, H_RET, DV_RET), state,
      wgn.reshape(H_RET, DV_RET), dmask.reshape(H_RET, 1), qdec.reshape(H_RET, DV_RET), kdec, cdec)
    return y.reshape(b, W_RET_V), so


def _att_prompt_kernel(*refs, seq, unroll):
    T = Q_TILE
    n_br = len(BRANCHES)
    q_refs, k_refs, v_refs = refs[:n_br], refs[n_br:2 * n_br], refs[2 * n_br:3 * n_br]
    bias_ref, o_ref = refs[3 * n_br], refs[3 * n_br + 1]
    part_refs, tmp_ref = refs[3 * n_br + 2:-1], refs[-1]
    head0 = lax.broadcasted_iota(jnp.int32, (1, LANES), 1) < HD_ATT
    n_tiles = seq // T

    def tile(bi, it):
        nblk = seq // (T * BRANCHES[bi][1])
        first = lax.rem(it, nblk) == 0
        qs = pl.multiple_of(it * T, T)
        ks = pl.multiple_of(jnp.where(first, it, it - 1) * T, T)
        bias = bias_ref[jnp.where(first, 0, 1)]
        q = q_refs[bi][pl.ds(qs, T), :]
        kk = k_refs[bi][pl.ds(ks, 2 * T), :]
        vv = v_refs[bi][pl.ds(ks, 2 * T), :]
        zero = jnp.zeros_like(q)
        q2 = jnp.concatenate([jnp.where(head0, q, zero), jnp.where(head0, zero, q)], axis=0)
        s = _dot_nt(q2, kk)
        ps, ms, ls = [], [], []
        for h in range(2):
            sh = s[h * T:(h + 1) * T] + bias
            mh = jnp.max(sh, axis=-1, keepdims=True)
            ph = jnp.exp2(sh - mh)
            ps.append(ph.astype(BF16))
            ms.append(mh)
            ls.append(jnp.sum(ph, axis=-1, keepdims=True))
        pv = _dot(jnp.concatenate(ps, axis=0), vv)
        l = jnp.where(head0, ls[0], ls[1])
        out = jnp.where(head0, pv[:T], pv[T:]) / l
        lse = jnp.where(head0, ms[0], ms[1]) + jnp.log2(l)
        return qs, out, lse

    for bi in range(n_br - 1, 0, -1):
        out_ref, lse_ref = part_refs[2 * (bi - 1)], part_refs[2 * (bi - 1) + 1]

        def body(g, carry, bi=bi, out_ref=out_ref, lse_ref=lse_ref):
            for u in range(unroll):
                qs, out, lse = tile(bi, g * unroll + u)
                out_ref[pl.ds(qs, T), :] = out
                lse_ref[pl.ds(qs, T), :] = lse
            return carry

        lax.fori_loop(0, n_tiles // unroll, body, 0)

    def body1(g, carry):
        for u in range(unroll):
            it = g * unroll + u
            qs, out, lse = tile(0, it)
            outs, lses = [out], [lse]
            for bi in range(1, n_br):
                d = BRANCHES[bi][1]
                per = T // d
                for j, src_ref in enumerate(part_refs[2 * (bi - 1):2 * bi]):
                    slot = 2 * (bi - 1) + j
                    for r in range(d):
                        src = pl.ds(pl.multiple_of(r * (seq // d) + it * per, per), per)
                        tmp_ref[u, slot, pl.ds(r, per, stride=d), :] = src_ref[src, :]
                outs.append(tmp_ref[u, 2 * (bi - 1)])
                lses.append(tmp_ref[u, 2 * (bi - 1) + 1])
            top = functools.reduce(jnp.maximum, lses)
            wts = [jnp.exp2(e - top) for e in lses]
            num = functools.reduce(lambda a, b: a + b, [w * o for w, o in zip(wts, outs)])
            den = functools.reduce(lambda a, b: a + b, wts)
            o_ref[pl.ds(qs, T), :] = (num / den).astype(o_ref.dtype)
        return carry

    lax.fori_loop(0, n_tiles // unroll, body1, 0)


def _att_bias():
    T = Q_TILE
    i = np.arange(T)[:, None]
    c = np.arange(2 * T)[None, :]
    dist = np.stack([i - c, i - c + T])
    return jnp.asarray(np.where((dist >= 0) & (dist <= T), 0.0, -np.inf).astype(np.float32))


def _att_prompt(att, batch, seq, *, unroll):
    assert BRANCHES[0][1] == 1 and all(w // d == Q_TILE for w, d in BRANCHES)
    T = Q_TILE
    nhp = W_ATT // LANES
    n_dil = len(BRANCHES) - 1
    blk = lambda b, h: (b, h)
    return pl.pallas_call(
        functools.partial(_att_prompt_kernel, seq=seq, unroll=unroll),
        grid=(batch, nhp),
        in_specs=[pl.BlockSpec((seq, LANES), blk)] * len(att)
        + [pl.BlockSpec((2, T, 2 * T), lambda b, h: (0, 0, 0))],
        out_specs=pl.BlockSpec((seq, LANES), blk),
        out_shape=jax.ShapeDtypeStruct((batch * seq, W_ATT), BF16),
        scratch_shapes=[pltpu.VMEM((seq, LANES), F32)] * (2 * n_dil)
        + [pltpu.VMEM((unroll, 2 * n_dil, T, LANES), F32)],
        compiler_params=_cparams(2),
        name="att_prompt",
    )(*att, _att_bias())


def _att_counts(past):
    dist = past - np.arange(past)
    cnt = sum(((dist % d == 0) & (dist <= w)).astype(np.float32) for w, d in BRANCHES)
    return jnp.asarray(cnt.reshape(1, past))


def _sample_attention(qc, knc, vnc, cnt, k_ref, v_ref):
    live = cnt > 0.0
    n_br = float(len(BRANCHES))
    s_rows, sn_rows = [], []
    for h in range(H_ATT):
        rows = slice(h * HD_ATT, (h + 1) * HD_ATT)
        s_rows.append(jnp.sum(k_ref[rows, :] * qc[rows], axis=0, keepdims=True))
        sn_rows.append(jnp.sum(knc[rows] * qc[rows], axis=0, keepdims=True))
    sc = jnp.where(live, jnp.concatenate(s_rows, axis=0), -jnp.inf)
    s_new = jnp.concatenate(sn_rows, axis=0)
    m = jnp.maximum(jnp.max(sc, axis=-1, keepdims=True), s_new)
    p = cnt * jnp.exp(sc - m)
    p_new = n_br * jnp.exp(s_new - m)
    den = jnp.sum(p, axis=-1, keepdims=True) + p_new
    outs = []
    for h in range(H_ATT):
        rows = slice(h * HD_ATT, (h + 1) * HD_ATT)
        acc = jnp.sum(v_ref[rows, :] * p[h:h + 1, :], axis=-1, keepdims=True) + p_new[h:h + 1] * vnc[rows]
        outs.append(acc / den[h:h + 1])
    return jnp.concatenate(outs, axis=0)


def _ffn_parts(x_ref, yr_ref, ya_ref, wo_ref, g2_ref, wg_ref, wu_ref, wd_ref, gf_ref, o_ref, final):
    st = {}

    def part(j):
        c0, n = FF_CHUNKS[j]
        if j == 0:
            mix = (_dot(yr_ref[...].astype(BF16), wo_ref[:W_RET_V, :])
                   + _dot(ya_ref[...].astype(BF16), wo_ref[W_RET_V:, :]))
            st["x1"] = x_ref[...] + mix
            st["xn"] = _rms(st["x1"], g2_ref[...]).astype(BF16)
        gate = _dot(st["xn"], wg_ref[:, c0:c0 + n])
        up = _dot(st["xn"], wu_ref[:, c0:c0 + n])
        down = _dot((_silu(gate) * up).astype(BF16), wd_ref[c0:c0 + n, :])
        st["acc"] = down if j == 0 else st["acc"] + down
        if j == len(FF_CHUNKS) - 1:
            x2 = st["x1"] + st["acc"]
            o_ref[...] = _rms(x2, gf_ref[...]) if final else x2

    return [functools.partial(part, j) for j in range(len(FF_CHUNKS))]


def _out_ffn_kernel(*refs, final):
    for part in _ffn_parts(*refs, final):
        part()


def _out_ffn_att_kernel(x_ref, yr_ref, ya_ref, wo_ref, g2_ref, wg_ref, wu_ref, wd_ref, gf_ref,
                        qc_ref, knc_ref, vnc_ref, cnt_ref, kt_hbm, vt_hbm,
                        o_ref, yc_ref, kbuf, vbuf, sem, *, final, layer, spp):
    i = pl.program_id(0)
    last_step = pl.num_programs(0) - 1
    n_slots = kbuf.shape[0]

    def copies(n, slot):
        return (pltpu.make_async_copy(kt_hbm.at[layer, n], kbuf.at[slot], sem.at[0, slot]),
                pltpu.make_async_copy(vt_hbm.at[layer, n], vbuf.at[slot], sem.at[1, slot]))

    def start(n, slot):
        for c in copies(n, slot):
            c.start()

    @pl.when(i == 0)
    def _():
        for slot in range(n_slots):
            start(slot, slot)

    parts = _ffn_parts(x_ref, yr_ref, ya_ref, wo_ref, g2_ref, wg_ref, wu_ref, wd_ref, gf_ref, o_ref, final)
    for a in range(spp):
        n = i * spp + a
        slot = a % n_slots
        for c in copies(n, slot):
            c.wait()
        col = slice(a, a + 1)
        yc_ref[:, col] = _sample_attention(qc_ref[:, col], knc_ref[:, col], vnc_ref[:, col], cnt_ref[...],
                                           kbuf.at[slot], vbuf.at[slot])
        parts[a]()
        if a + n_slots < spp:
            start(n + n_slots, slot)
        else:
            @pl.when(i < last_step)
            def _(n=n, slot=slot):
                start(n + n_slots, slot)


def _ffn_specs(tm):
    row = lambda i: (i, 0)
    const = lambda i: (0, 0)
    resident = functools.partial(pl.BlockSpec, index_map=const, pipeline_mode=pl.Buffered(1))
    return [
        pl.BlockSpec((tm, D_MODEL), row),
        pl.BlockSpec((tm, W_RET_V), row),
        pl.BlockSpec((tm, W_ATT), row),
        resident((W_RET_V + W_ATT, D_MODEL)),
        pl.BlockSpec((1, D_MODEL), const),
        resident((D_MODEL, D_FF)),
        resident((D_MODEL, D_FF)),
        resident((D_FF, D_MODEL)),
        pl.BlockSpec((1, D_MODEL), const),
    ]


def _out_ffn(x, yr, ya, wo, g2, wg, wu, wd, gf, *, tm, final):
    n = x.shape[0]
    return pl.pallas_call(
        functools.partial(_out_ffn_kernel, final=final),
        grid=(n // tm,),
        in_specs=_ffn_specs(tm),
        out_specs=pl.BlockSpec((tm, D_MODEL), lambda i: (i, 0)),
        out_shape=jax.ShapeDtypeStruct((n, D_MODEL), F32),
        compiler_params=_cparams(1),
        name="out_ffn",
    )(x, yr, ya, wo, g2.reshape(1, D_MODEL), wg, wu, wd, gf.reshape(1, D_MODEL))


def _out_ffn_att(x, yr, ya, wo, g2, wg, wu, wd, gf, qa, ka, va, kt, vt, layer, *, tm, final):
    n = x.shape[0]
    steps = n // tm
    b = qa.shape[0]
    past = kt.shape[-1]
    spp = b // steps
    assert b % steps == 0 and spp == len(FF_CHUNKS) and spp % 2 == 0
    cols = lambda t: jnp.transpose(t.reshape(steps, spp, W_ATT), (0, 2, 1))
    col_spec = pl.BlockSpec((None, W_ATT, spp), lambda i: (i, 0, 0))
    o, yc = pl.pallas_call(
        functools.partial(_out_ffn_att_kernel, final=final, layer=layer, spp=spp),
        grid=(steps,),
        in_specs=_ffn_specs(tm) + [
            col_spec, col_spec, col_spec,
            pl.BlockSpec((1, past), lambda i: (0, 0)),
            pl.BlockSpec(memory_space=pl.ANY),
            pl.BlockSpec(memory_space=pl.ANY),
        ],
        out_specs=[pl.BlockSpec((tm, D_MODEL), lambda i: (i, 0)), col_spec],
        out_shape=[jax.ShapeDtypeStruct((n, D_MODEL), F32),
                   jax.ShapeDtypeStruct((steps, W_ATT, spp), F32)],
        scratch_shapes=[pltpu.VMEM((2, W_ATT, past), F32), pltpu.VMEM((2, W_ATT, past), F32),
                        pltpu.SemaphoreType.DMA((2, 2))],
        compiler_params=_cparams(1),
        name="out_ffn_att",
    )(x, yr, ya, wo, g2.reshape(1, D_MODEL), wg, wu, wd, gf.reshape(1, D_MODEL),
      cols(qa), cols(ka), cols(va), _att_counts(past), kt, vt)
    return o, jnp.transpose(yc, (0, 2, 1)).reshape(b, W_ATT)


def kernel(x_prompt, x_sample, state_ret, cache_k, cache_v, norm1_g, w_in, w_gn, w_o, norm2_g,
           w_gate, w_up, w_down, final_g):
    batch, seq, _ = x_prompt.shape
    dec_batch, dec_seq, _ = x_sample.shape
    depth = w_in.shape[0]
    past = cache_k.shape[2]
    assert dec_seq == 1 and past == WINDOW_MAX
    assert all(seq % (Q_TILE * d) == 0 and seq // (Q_TILE * d) >= 2 for _, d in BRANCHES)
    keep = min(WINDOW_MAX, seq)

    tm = 512
    pos_p = jnp.arange(seq, dtype=F32)
    pos_s = jnp.full((dec_batch,), float(past), F32) + jnp.arange(dec_seq, dtype=F32)[0]
    tabs_p = _ret_tables(pos_p) + _rope_tables(pos_p)
    tabs_s = _ret_tables(pos_s) + _rope_tables(pos_s)

    w_in_b = w_in.astype(BF16)
    w_o_b = w_o.astype(BF16)
    w_gate_b = w_gate.astype(BF16)
    w_up_b = w_up.astype(BF16)
    w_down_b = w_down.astype(BF16)
    kt = jnp.transpose(cache_k, (0, 1, 3, 4, 2)).reshape(depth, dec_batch, W_ATT, past)
    vt = jnp.transpose(cache_v, (0, 1, 3, 4, 2)).reshape(depth, dec_batch, W_ATT, past)

    xp = x_prompt.reshape(batch * seq, D_MODEL)
    xs = x_sample.reshape(dec_batch, D_MODEL)
    st_p, kp_l, vp_l, st_s, ks_l, vs_l = [], [], [], [], [], []
    for l in range(depth):
        final = l == depth - 1
        ffn_w = (w_o_b[l], norm2_g[l], w_gate_b[l], w_up_b[l], w_down_b[l], final_g)

        (qr, kr, vr, gr), att, (kc, vc) = _in_proj_prompt(
            xp, norm1_g[l], w_in_b[l], tabs_p, batch, seq, tm=tm)
        y_ret, r_p = _ret_prompt(qr, kr, vr, gr, w_gn[l], batch, seq, tt=tm)
        y_att = _att_prompt(att, batch, seq, unroll=8)
        st_p.append(r_p.reshape(batch, H_RET, DK_RET, DV_RET))
        kp_l.append(jnp.transpose(kc.reshape(batch, H_ATT, HD_ATT, keep), (0, 3, 1, 2)))
        vp_l.append(jnp.transpose(vc.reshape(batch, H_ATT, HD_ATT, keep), (0, 3, 1, 2)))

        qr, kr, vr, gr, qa, ka, va = _in_proj_sample(xs, norm1_g[l], w_in_b[l], tabs_s)
        y_ret_s, r_s = _ret_sample(
            qr, kr, vr, gr, state_ret[l].reshape(dec_batch, W_RET_QK, DV_RET), w_gn[l], sb=16)
        xp, y_att_s = _out_ffn_att(xp, y_ret, y_att, *ffn_w, qa, ka, va, kt, vt, l, tm=tm, final=final)
        xs = _out_ffn(xs, y_ret_s, y_att_s, *ffn_w, tm=dec_batch, final=final)
        st_s.append(r_s.reshape(dec_batch, H_RET, DK_RET, DV_RET))
        ks_l.append(ka.reshape(dec_batch, dec_seq, H_ATT, HD_ATT))
        vs_l.append(va.reshape(dec_batch, dec_seq, H_ATT, HD_ATT))

    return (
        xp.reshape(batch, seq, D_MODEL),
        xs.reshape(dec_batch, dec_seq, D_MODEL),
        jnp.stack(st_p), jnp.stack(kp_l), jnp.stack(vp_l),
        jnp.stack(st_s), jnp.stack(ks_l), jnp.stack(vs_l),
    )
```

```python
import functools

import jax
import jax.numpy as jnp
import numpy as np
from jax import lax
from jax.experimental import pallas as pl
from jax.experimental.pallas import tpu as pltpu

F32 = jnp.float32
BF16 = jnp.bfloat16

D_MODEL = 1024
H_RET = 4
DK_RET = 64
DV_RET = 128
RET_CHUNK = 128
RET_THETA = 10000.0
H_ATT = 8
HD_ATT = 64
BRANCHES = ((128, 1), (512, 4), (2048, 16))
WINDOW_MAX = 2048
ROPE_THETA = 10000.0
W_RET_QK = H_RET * DK_RET
W_RET_V = H_RET * DV_RET
W_ATT = H_ATT * HD_ATT
IN_WIDTH = 2 * W_RET_QK + 2 * W_RET_V + 3 * W_ATT
D_FF = 2816
NORM_EPS = 1e-6
LOG2E = 1.4426950408889634

LANES = 128
Q_TILE = 128
VMEM_LIMIT = 56 * 1024 * 1024

FF_CHUNKS = ((0, 256), (256, 1024), (1280, 1024), (2304, 512))


def _cparams(n_axes):
    return pltpu.CompilerParams(
        dimension_semantics=("arbitrary",) * n_axes, vmem_limit_bytes=VMEM_LIMIT
    )


def _rms(x, g):
    return (x * lax.rsqrt(jnp.mean(x * x, axis=-1, keepdims=True) + NORM_EPS)) * g


def _silu(x):
    return x * (1.0 / (1.0 + jnp.exp(-x)))


def _dot(a, b):
    return jnp.dot(a, b, preferred_element_type=F32)


def _dot_nt(a, b):
    return lax.dot_general(a, b, (((1,), (1,)), ((), ())), preferred_element_type=F32)


def _dot_tn(a, b):
    return lax.dot_general(a, b, (((0,), (0,)), ((), ())), preferred_element_type=F32)


def _ret_tables(pos):
    angle = 1.0 / (RET_THETA ** jnp.linspace(0.0, 1.0, DK_RET // 2, dtype=F32))
    angle = jnp.repeat(angle, 2)
    ang = pos[:, None] * angle[None, :]
    cos, sin = jnp.cos(ang), jnp.sin(ang)
    even = (jnp.arange(DK_RET) % 2) == 0
    ssin = jnp.where(even[None, :], -sin, sin)
    return jnp.tile(cos, (1, 2)), jnp.tile(ssin, (1, 2))


def _rope_tables(pos):
    half = HD_ATT // 2
    inv = ROPE_THETA ** (-2.0 * jnp.arange(half, dtype=F32) / HD_ATT)
    ang = pos[:, None] * inv[None, :]
    cos, sin = jnp.cos(ang), jnp.sin(ang)
    cosf = jnp.concatenate([cos, cos], axis=-1)
    ssin = jnp.concatenate([-sin, sin], axis=-1)
    return jnp.tile(cosf, (1, 2)), jnp.tile(ssin, (1, 2))


def _decay_tables(C):
    log_gamma = jnp.log1p(-jnp.exp2(-5.0 - jnp.arange(H_RET, dtype=F32)))
    idx = jnp.arange(C, dtype=F32)
    diff = idx[:, None] - idx[None, :]
    dmask = jnp.where(diff >= 0, jnp.exp(jnp.maximum(diff, 0.0) * log_gamma[:, None, None]), 0.0)
    q_decay = jnp.exp((idx + 1.0)[None, :] * log_gamma[:, None])
    k_decay = jnp.exp((C - 1.0 - idx)[None, :] * log_gamma[:, None])
    chunk_decay = jnp.exp(C * log_gamma)
    qdec = jnp.broadcast_to(q_decay[:, :, None], (H_RET, C, DV_RET))
    kdec = jnp.repeat(k_decay.T, DK_RET, axis=1)
    cdec = jnp.broadcast_to(jnp.repeat(chunk_decay, DK_RET)[:, None], (W_RET_QK, DV_RET))
    return dmask, qdec, kdec, cdec


def _in_proj_kernel(x_ref, g_ref, w_ref, rc_ref, rs_ref, ac_ref, as_ref, *rest, prompt, n_carried=0):
    qr_ref, kr_ref, vr_ref, gr_ref, *rest = rest[n_carried:]
    xn = _rms(x_ref[...], g_ref[...]).astype(BF16)
    lane = lax.broadcasted_iota(jnp.int32, (1, LANES), 1)
    even = (lane & 1) == 0
    low = (lane & (HD_ATT - 1)) < HD_ATT // 2

    def proj(c0, n):
        return _dot(xn, w_ref[:, c0:c0 + n])

    def rot_pairs(y):
        partner = jnp.where(even, pltpu.roll(y, LANES - 1, 1), pltpu.roll(y, 1, 1))
        return y * rc_ref[...] + partner * rs_ref[...]

    def rot_half(y):
        h = HD_ATT // 2
        partner = jnp.where(low, pltpu.roll(y, LANES - h, 1), pltpu.roll(y, h, 1))
        return y * ac_ref[...] + partner * as_ref[...]

    o_kr = W_RET_QK
    o_vr = o_kr + W_RET_QK
    o_gr = o_vr + W_RET_V
    o_qa = o_gr + W_RET_V
    o_ka = o_qa + W_ATT
    o_va = o_ka + W_ATT
    n_slab = W_ATT // LANES
    q_scale = HD_ATT ** -0.5

    def retention_part():
        y = proj(0, 2 * W_RET_QK)
        for c in range(W_RET_QK // LANES):
            sl = slice(c * LANES, (c + 1) * LANES)
            qr_ref[:, sl] = rot_pairs(y[:, sl]).astype(qr_ref.dtype)
            k = rot_pairs(y[:, o_kr + c * LANES:o_kr + (c + 1) * LANES]) * (DK_RET ** -0.5)
            kr_ref[:, sl] = k.astype(kr_ref.dtype)
        vr_ref[...] = proj(o_vr, W_RET_V).astype(vr_ref.dtype)
        gr_ref[...] = proj(o_gr, W_RET_V).astype(gr_ref.dtype)

    if not prompt:
        ka_ref, va_ref, qat_ref, kat_ref, vat_ref = rest
        retention_part()
        yq = proj(o_qa, W_ATT)
        yk = proj(o_ka, W_ATT)
        yv = proj(o_va, W_ATT)
        for c in range(n_slab):
            sl = slice(c * LANES, (c + 1) * LANES)
            q = rot_half(yq[:, sl]) * q_scale
            k = rot_half(yk[:, sl])
            ka_ref[:, sl] = k
            va_ref[:, sl] = yv[:, sl]
            qat_ref[sl, :] = q.T
            kat_ref[sl, :] = k.T
            vat_ref[sl, :] = yv[:, sl].T
        return

    att_refs, (kc_ref, vc_ref, ys_ref) = rest[:-3], rest[-3:]
    n_dil = len(BRANCHES)
    tm = x_ref.shape[0]
    dils = [d for _, d in BRANCHES]

    def emit(slab_fn, refs, cache_ref):
        for c in range(n_slab):
            sl = slice(c * LANES, (c + 1) * LANES)
            yc = slab_fn(c)
            ys_ref[0, c] = yc
            refs[0][0, :, sl] = yc.astype(refs[0].dtype)
            if cache_ref is not None:
                cache_ref[sl, :] = yc.T
        for i in range(1, n_dil):
            d0, d1 = dils[i - 1], dils[i]
            step = d1 // d0
            for r0 in range(d0):
                for r1 in range(step):
                    r = r0 + d0 * r1
                    for c in range(n_slab):
                        sl = slice(c * LANES, (c + 1) * LANES)
                        yc = ys_ref[i - 1, c, pl.ds(r0 * (tm // d0) + r1, tm // d1, stride=step), :]
                        if i < n_dil - 1:
                            ys_ref[i, c, r * (tm // d1):(r + 1) * (tm // d1), :] = yc
                        refs[i][r, :, sl] = yc.astype(refs[i].dtype)

    y = proj(o_ka, W_ATT)
    emit(lambda c: rot_half(y[:, c * LANES:(c + 1) * LANES]), att_refs[n_dil:2 * n_dil], kc_ref)
    y = proj(o_va, W_ATT)
    emit(lambda c: y[:, c * LANES:(c + 1) * LANES], att_refs[2 * n_dil:], vc_ref)
    y = proj(o_qa, W_ATT)
    emit(lambda c: rot_half(y[:, c * LANES:(c + 1) * LANES]) * (q_scale * LOG2E), att_refs[:n_dil], None)
    retention_part()


def _in_proj_specs(tm):
    const = lambda i: (0, 0)
    return [
        pl.BlockSpec((tm, D_MODEL), lambda i: (i, 0)),
        pl.BlockSpec((1, D_MODEL), const),
        pl.BlockSpec((D_MODEL, IN_WIDTH), const),
    ]


def _in_proj_sample(x, g, w, tabs):
    n = x.shape[0]
    row = lambda i: (i, 0)
    widths = [W_RET_QK, W_RET_QK, W_RET_V, W_RET_V, W_ATT, W_ATT]
    return pl.pallas_call(
        functools.partial(_in_proj_kernel, prompt=False),
        grid=(1,),
        in_specs=_in_proj_specs(n) + [pl.BlockSpec((n, LANES), row)] * 4,
        out_specs=[pl.BlockSpec((n, wd), row) for wd in widths] + [pl.BlockSpec((W_ATT, n), row)] * 3,
        out_shape=[jax.ShapeDtypeStruct((n, wd), F32) for wd in widths]
        + [jax.ShapeDtypeStruct((W_ATT, n), F32)] * 3,
        compiler_params=_cparams(1),
        name="in_proj_sample",
    )(x, g.reshape(1, D_MODEL), w, *tabs)


def _in_proj_prompt(x, g, w, tabs, batch, seq, layer, depth, caches, *, tm):
    n = batch * seq
    nt = seq // tm
    keep = min(WINDOW_MAX, seq)
    row = lambda i: (i, 0)
    tab = lambda i: (i % nt, 0)
    ret_out = [W_RET_QK, W_RET_QK, W_RET_V, W_RET_V]
    out_specs = [pl.BlockSpec((tm, wd), row) for wd in ret_out]
    out_shape = [jax.ShapeDtypeStruct((n, wd), BF16) for wd in ret_out]
    for _ in range(3):
        for _, d in BRANCHES:
            out_specs.append(pl.BlockSpec((None, d, tm // d, W_ATT), lambda i: (i // nt, 0, i % nt, 0)))
            out_shape.append(jax.ShapeDtypeStruct((batch, d, seq // d, W_ATT), BF16))
    first_kept = (seq - keep) // tm
    for _ in range(2):
        out_specs.append(pl.BlockSpec((None, None, W_ATT, tm),
                                      lambda i: (layer, i // nt, 0, jnp.maximum(i % nt - first_kept, 0))))
        out_shape.append(jax.ShapeDtypeStruct((depth, batch, W_ATT, keep), F32))
    carried = list(caches)
    n_in = 3 + 4
    outs = pl.pallas_call(
        functools.partial(_in_proj_kernel, prompt=True, n_carried=len(carried)),
        grid=(n // tm,),
        in_specs=_in_proj_specs(tm) + [pl.BlockSpec((tm, LANES), tab)] * 4
        + [pl.BlockSpec(memory_space=pl.ANY)] * len(carried),
        out_specs=out_specs,
        out_shape=out_shape,
        input_output_aliases={n_in + j: len(out_shape) - len(carried) + j for j in range(len(carried))},
        scratch_shapes=[pltpu.VMEM((len(BRANCHES) - 1, W_ATT // LANES, tm, LANES), F32)],
        compiler_params=_cparams(1),
        name="in_proj_prompt",
    )(x, g.reshape(1, D_MODEL), w, *tabs, *carried)
    att = [o.reshape(n, W_ATT) for o in outs[4:4 + 3 * len(BRANCHES)]]
    return outs[:4], att, outs[-2:]


def _group_norm_gate(o, gate, wgn):
    mu = jnp.mean(o, axis=-1, keepdims=True)
    var = jnp.mean(jnp.square(o - mu), axis=-1, keepdims=True)
    yn = (o - mu) * lax.rsqrt(var + NORM_EPS)
    return _silu(gate) * (yn * wgn)


def _ret_prompt_kernel(q_ref, k_ref, v_ref, g_ref, wgn_ref, dmask_ref, qdec_ref, kdec_ref, cdec_ref,
                       y_ref, st_ref, r_ref, *, n_chunks):
    t = pl.program_id(1)

    @pl.when(t == 0)
    def _():
        r_ref[...] = jnp.zeros_like(r_ref)

    head_of_lane = lax.broadcasted_iota(jnp.int32, (1, W_RET_QK), 1) // DK_RET
    C = RET_CHUNK
    for c in range(n_chunks):
        rows = slice(c * C, (c + 1) * C)
        q = q_ref[rows, :]
        k = k_ref[rows, :]
        v = v_ref[rows, :]
        qm = jnp.concatenate(
            [jnp.where(head_of_lane == h, q, jnp.zeros_like(q)) for h in range(H_RET)], axis=0)
        s_all = _dot_nt(qm, k)
        x_all = _dot(qm, r_ref[...].astype(BF16))
        for h in range(H_RET):
            hr = slice(h * C, (h + 1) * C)
            hv = slice(h * DV_RET, (h + 1) * DV_RET)
            s = s_all[hr] * dmask_ref[h]
            o = _dot(s.astype(BF16), v[:, hv]) + x_all[hr] * qdec_ref[h]
            y = _group_norm_gate(o, g_ref[rows, hv].astype(F32), wgn_ref[:, hv])
            y_ref[rows, hv] = y.astype(y_ref.dtype)
        kd = (k.astype(F32) * kdec_ref[...]).astype(BF16)
        kv = _dot_tn(kd, v)
        for h in range(H_RET):
            hk = slice(h * DK_RET, (h + 1) * DK_RET)
            hv = slice(h * DV_RET, (h + 1) * DV_RET)
            r_ref[hk, :] = r_ref[hk, :] * cdec_ref[hk, :] + kv[hk, hv]

    @pl.when(t == pl.num_programs(1) - 1)
    def _():
        st_ref[...] = r_ref[...]


def _ret_prompt(qr, kr, vr, gr, wgn, batch, seq, *, tt):
    dmask, qdec, kdec, cdec = _decay_tables(RET_CHUNK)
    nt = seq // tt
    row = lambda b, t: (b * nt + t, 0)
    c2 = lambda b, t: (0, 0)
    c3 = lambda b, t: (0, 0, 0)
    return pl.pallas_call(
        functools.partial(_ret_prompt_kernel, n_chunks=tt // RET_CHUNK),
        grid=(batch, nt),
        in_specs=[
            pl.BlockSpec((tt, W_RET_QK), row),
            pl.BlockSpec((tt, W_RET_QK), row),
            pl.BlockSpec((tt, W_RET_V), row),
            pl.BlockSpec((tt, W_RET_V), row),
            pl.BlockSpec((1, W_RET_V), c2),
            pl.BlockSpec((H_RET, RET_CHUNK, RET_CHUNK), c3),
            pl.BlockSpec((H_RET, RET_CHUNK, DV_RET), c3),
            pl.BlockSpec((RET_CHUNK, W_RET_QK), c2),
            pl.BlockSpec((W_RET_QK, DV_RET), c2),
        ],
        out_specs=[
            pl.BlockSpec((tt, W_RET_V), row),
            pl.BlockSpec((None, W_RET_QK, DV_RET), lambda b, t: (b, 0, 0)),
        ],
        out_shape=[
            jax.ShapeDtypeStruct((batch * seq, W_RET_V), BF16),
            jax.ShapeDtypeStruct((batch, W_RET_QK, DV_RET), F32),
        ],
        scratch_shapes=[pltpu.VMEM((W_RET_QK, DV_RET), F32)],
        compiler_params=_cparams(2),
        name="ret_prompt",
    )(qr, kr, vr, gr, wgn.reshape(1, W_RET_V), dmask, qdec, kdec, cdec)


def _ret_sample_kernel(q_ref, k_ref, v_ref, g_ref, st_ref, wgn_ref, dmask_ref, qdec_ref, kdec_ref, cdec_ref,
                       *rest, sb):
    y_ref, so_ref = rest[-2:]
    head_row = lax.broadcasted_iota(jnp.int32, (H_RET, W_RET_QK), 0)
    head_lane = lax.broadcasted_iota(jnp.int32, (H_RET, W_RET_QK), 1) // DK_RET
    own = head_row == head_lane
    for s in range(sb):
        qm = jnp.where(own, q_ref[s:s + 1, :], 0.0)
        km = jnp.where(own, k_ref[s:s + 1, :], 0.0)
        r = st_ref[s]
        x = _dot(qm.astype(BF16), r.astype(BF16))
        sc = jnp.sum(qm * km, axis=-1, keepdims=True) * dmask_ref[...]
        v4 = v_ref[s]
        o = sc * v4 + x * qdec_ref[...]
        y_ref[s] = _group_norm_gate(o, g_ref[s], wgn_ref[...])
        kv = _dot_tn((km * kdec_ref[...]).astype(BF16), v4.astype(BF16))
        so_ref[s] = r * cdec_ref[...] + kv


def _ret_sample(qr, kr, vr, gr, state, wgn, layer, new_state, *, sb):
    b = qr.shape[0]
    depth = state.shape[0]
    carried = [new_state]
    st3 = lambda i: (layer, i, 0, 0)
    dmask, qdec, kdec, cdec = _decay_tables(1)
    row = lambda i: (i, 0)
    row3 = lambda i: (i, 0, 0)
    c2 = lambda i: (0, 0)
    y, so = pl.pallas_call(
        functools.partial(_ret_sample_kernel, sb=sb),
        grid=(b // sb,),
        in_specs=[
            pl.BlockSpec((sb, W_RET_QK), row),
            pl.BlockSpec((sb, W_RET_QK), row),
            pl.BlockSpec((sb, H_RET, DV_RET), row3),
            pl.BlockSpec((sb, H_RET, DV_RET), row3),
            pl.BlockSpec((None, sb, W_RET_QK, DV_RET), st3),
            pl.BlockSpec((H_RET, DV_RET), c2),
            pl.BlockSpec((H_RET, 1), c2),
            pl.BlockSpec((H_RET, DV_RET), c2),
            pl.BlockSpec((1, W_RET_QK), c2),
            pl.BlockSpec((W_RET_QK, DV_RET), c2),
        ] + [pl.BlockSpec(memory_space=pl.ANY)] * len(carried),
        out_specs=[
            pl.BlockSpec((sb, H_RET, DV_RET), row3),
            pl.BlockSpec((None, sb, W_RET_QK, DV_RET), st3),
        ],
        out_shape=[
            jax.ShapeDtypeStruct((b, H_RET, DV_RET), F32),
            jax.ShapeDtypeStruct((depth, b, W_RET_QK, DV_RET), F32),
        ],
        input_output_aliases={10: 1},
        compiler_params=_cparams(1),
        name="ret_sample",
    )(qr, kr, vr.reshape(b, H_RET, DV_RET), gr.reshape(b, H_RET, DV_RET), state,
      wgn.reshape(H_RET, DV_RET), dmask.reshape(H_RET, 1), qdec.reshape(H_RET, DV_RET), kdec, cdec, *carried)
    return y.reshape(b, W_RET_V), so


def _att_prompt_kernel(*refs, seq, unroll):
    T = Q_TILE
    n_br = len(BRANCHES)
    q_refs, k_refs, v_refs = refs[:n_br], refs[n_br:2 * n_br], refs[2 * n_br:3 * n_br]
    bias_ref, o_ref = refs[3 * n_br], refs[3 * n_br + 1]
    part_refs, tmp_ref = refs[3 * n_br + 2:-1], refs[-1]
    head0 = lax.broadcasted_iota(jnp.int32, (1, LANES), 1) < HD_ATT
    n_tiles = seq // T

    def tile(bi, it):
        nblk = seq // (T * BRANCHES[bi][1])
        first = lax.rem(it, nblk) == 0
        qs = pl.multiple_of(it * T, T)
        ks = pl.multiple_of(jnp.where(first, it, it - 1) * T, T)
        bias = bias_ref[jnp.where(first, 0, 1)]
        q = q_refs[bi][pl.ds(qs, T), :]
        kk = k_refs[bi][pl.ds(ks, 2 * T), :]
        vv = v_refs[bi][pl.ds(ks, 2 * T), :]
        zero = jnp.zeros_like(q)
        q2 = jnp.concatenate([jnp.where(head0, q, zero), jnp.where(head0, zero, q)], axis=0)
        s = _dot_nt(q2, kk)
        ps, ms, ls = [], [], []
        for h in range(2):
            sh = s[h * T:(h + 1) * T] + bias
            mh = jnp.max(sh, axis=-1, keepdims=True)
            ph = jnp.exp2(sh - mh)
            ps.append(ph.astype(BF16))
            ms.append(mh)
            ls.append(jnp.sum(ph, axis=-1, keepdims=True))
        pv = _dot(jnp.concatenate(ps, axis=0), vv)
        l = jnp.where(head0, ls[0], ls[1])
        out = jnp.where(head0, pv[:T], pv[T:]) / l
        lse = jnp.where(head0, ms[0], ms[1]) + jnp.log2(l)
        return qs, out, lse

    for bi in range(n_br - 1, 0, -1):
        out_ref, lse_ref = part_refs[2 * (bi - 1)], part_refs[2 * (bi - 1) + 1]

        def body(g, carry, bi=bi, out_ref=out_ref, lse_ref=lse_ref):
            for u in range(unroll):
                qs, out, lse = tile(bi, g * unroll + u)
                out_ref[pl.ds(qs, T), :] = out
                lse_ref[pl.ds(qs, T), :] = lse
            return carry

        lax.fori_loop(0, n_tiles // unroll, body, 0)

    def body1(g, carry):
        for u in range(unroll):
            it = g * unroll + u
            qs, out, lse = tile(0, it)
            outs, lses = [out], [lse]
            for bi in range(1, n_br):
                d = BRANCHES[bi][1]
                per = T // d
                for j, src_ref in enumerate(part_refs[2 * (bi - 1):2 * bi]):
                    slot = 2 * (bi - 1) + j
                    for r in range(d):
                        src = pl.ds(pl.multiple_of(r * (seq // d) + it * per, per), per)
                        tmp_ref[u, slot, pl.ds(r, per, stride=d), :] = src_ref[src, :]
                outs.append(tmp_ref[u, 2 * (bi - 1)])
                lses.append(tmp_ref[u, 2 * (bi - 1) + 1])
            top = functools.reduce(jnp.maximum, lses)
            wts = [jnp.exp2(e - top) for e in lses]
            num = functools.reduce(lambda a, b: a + b, [w * o for w, o in zip(wts, outs)])
            den = functools.reduce(lambda a, b: a + b, wts)
            o_ref[pl.ds(qs, T), :] = (num / den).astype(o_ref.dtype)
        return carry

    lax.fori_loop(0, n_tiles // unroll, body1, 0)


def _att_bias():
    T = Q_TILE
    i = np.arange(T)[:, None]
    c = np.arange(2 * T)[None, :]
    dist = np.stack([i - c, i - c + T])
    return jnp.asarray(np.where((dist >= 0) & (dist <= T), 0.0, -np.inf).astype(np.float32))


def _att_prompt(att, batch, seq, *, unroll):
    assert BRANCHES[0][1] == 1 and all(w // d == Q_TILE for w, d in BRANCHES)
    assert all(b[1] % a[1] == 0 for a, b in zip(BRANCHES, BRANCHES[1:]))
    T = Q_TILE
    nhp = W_ATT // LANES
    n_dil = len(BRANCHES) - 1
    blk = lambda b, h: (b, h)
    return pl.pallas_call(
        functools.partial(_att_prompt_kernel, seq=seq, unroll=unroll),
        grid=(batch, nhp),
        in_specs=[pl.BlockSpec((seq, LANES), blk)] * len(att)
        + [pl.BlockSpec((2, T, 2 * T), lambda b, h: (0, 0, 0))],
        out_specs=pl.BlockSpec((seq, LANES), blk),
        out_shape=jax.ShapeDtypeStruct((batch * seq, W_ATT), BF16),
        scratch_shapes=[pltpu.VMEM((seq, LANES), F32)] * (2 * n_dil)
        + [pltpu.VMEM((unroll, 2 * n_dil, T, LANES), F32)],
        compiler_params=_cparams(2),
        name="att_prompt",
    )(*att, _att_bias())


def _att_counts(past):
    dist = past - np.arange(past)
    cnt = sum(((dist % d == 0) & (dist <= w)).astype(np.float32) for w, d in BRANCHES)
    return jnp.asarray(cnt.reshape(1, past))


def _sample_attention(qc, knc, vnc, cnt, k_ref, v_ref):
    live = cnt > 0.0
    n_br = float(len(BRANCHES))
    s_rows, sn_rows = [], []
    for h in range(H_ATT):
        rows = slice(h * HD_ATT, (h + 1) * HD_ATT)
        s_rows.append(jnp.sum(k_ref[rows, :] * qc[rows], axis=0, keepdims=True))
        sn_rows.append(jnp.sum(knc[rows] * qc[rows], axis=0, keepdims=True))
    sc = jnp.where(live, jnp.concatenate(s_rows, axis=0), -jnp.inf)
    s_new = jnp.concatenate(sn_rows, axis=0)
    m = jnp.maximum(jnp.max(sc, axis=-1, keepdims=True), s_new)
    p = cnt * jnp.exp(sc - m)
    p_new = n_br * jnp.exp(s_new - m)
    den = jnp.sum(p, axis=-1, keepdims=True) + p_new
    outs = []
    for h in range(H_ATT):
        rows = slice(h * HD_ATT, (h + 1) * HD_ATT)
        acc = jnp.sum(v_ref[rows, :] * p[h:h + 1, :], axis=-1, keepdims=True) + p_new[h:h + 1] * vnc[rows]
        outs.append(acc / den[h:h + 1])
    return jnp.concatenate(outs, axis=0)


def _ffn_parts(x_ref, yr_ref, ya_ref, wo_ref, g2_ref, wg_ref, wu_ref, wd_ref, gf_ref, o_ref, final, ya_t=False):
    st = {}
    last = len(FF_CHUNKS) - 1

    def down(j):
        c0, n = FF_CHUNKS[j]
        d = _dot(st.pop(("h", j)), wd_ref[c0:c0 + n, :])
        st["acc"] = d if j == 0 else st["acc"] + d

    def part(j):
        c0, n = FF_CHUNKS[j]
        if j == 0:
            att_dot = _dot_tn if ya_t else _dot
            mix = (_dot(yr_ref[...].astype(BF16), wo_ref[:W_RET_V, :])
                   + att_dot(ya_ref[...].astype(BF16), wo_ref[W_RET_V:, :]))
            st["x1"] = x_ref[...] + mix
            st["xn"] = _rms(st["x1"], g2_ref[...]).astype(BF16)
        gate = _dot(st["xn"], wg_ref[:, c0:c0 + n])
        up = _dot(st["xn"], wu_ref[:, c0:c0 + n])
        if j > 0:
            down(j - 1)
        st[("h", j)] = (_silu(gate) * up).astype(BF16)
        if j == last:
            down(j)
            x2 = st["x1"] + st["acc"]
            o_ref[...] = _rms(x2, gf_ref[...]) if final else x2

    return [functools.partial(part, j) for j in range(len(FF_CHUNKS))]


def _out_ffn_kernel(*refs, final, ya_t):
    for part in _ffn_parts(*refs, final, ya_t):
        part()


def _out_ffn_att_kernel(x_ref, yr_ref, ya_ref, wo_ref, g2_ref, wg_ref, wu_ref, wd_ref, gf_ref,
                        qt_ref, knt_ref, vnt_ref, cnt_ref, kt_hbm, vt_hbm,
                        o_ref, yt_ref, kbuf, vbuf, sem, *, final, layer, spp):
    i = pl.program_id(0)
    last_step = pl.num_programs(0) - 1
    n_slots = kbuf.shape[0]

    def copies(n, slot):
        return (pltpu.make_async_copy(kt_hbm.at[layer, n], kbuf.at[slot], sem.at[0, slot]),
                pltpu.make_async_copy(vt_hbm.at[layer, n], vbuf.at[slot], sem.at[1, slot]))

    def start(n, slot):
        for c in copies(n, slot):
            c.start()

    @pl.when(i == 0)
    def _():
        for slot in range(n_slots):
            start(slot, slot)
        yt_ref[...] = jnp.zeros_like(yt_ref)

    lane = lax.broadcasted_iota(jnp.int32, (1, qt_ref.shape[1]), 1)

    parts = _ffn_parts(x_ref, yr_ref, ya_ref, wo_ref, g2_ref, wg_ref, wu_ref, wd_ref, gf_ref, o_ref, final)
    for a in range(spp):
        n = i * spp + a
        slot = a % n_slots
        for c in copies(n, slot):
            c.wait()
        mine = lane == n
        column = lambda ref: jnp.sum(jnp.where(mine, ref[...], 0.0), axis=-1, keepdims=True)
        y = _sample_attention(column(qt_ref), column(knt_ref), column(vnt_ref), cnt_ref[...],
                              kbuf.at[slot], vbuf.at[slot])
        yt_ref[...] = jnp.where(mine, y, yt_ref[...])
        parts[a]()
        if a + n_slots < spp:
            start(n + n_slots, slot)
        else:
            @pl.when(i < last_step)
            def _(n=n, slot=slot):
                start(n + n_slots, slot)


def _ffn_specs(tm, ya_t=False):
    row = lambda i: (i, 0)
    const = lambda i: (0, 0)
    resident = functools.partial(pl.BlockSpec, index_map=const, pipeline_mode=pl.Buffered(1))
    return [
        pl.BlockSpec((tm, D_MODEL), row),
        pl.BlockSpec((tm, W_RET_V), row),
        pl.BlockSpec((W_ATT, tm), lambda i: (0, i)) if ya_t else pl.BlockSpec((tm, W_ATT), row),
        resident((W_RET_V + W_ATT, D_MODEL)),
        pl.BlockSpec((1, D_MODEL), const),
        resident((D_MODEL, D_FF)),
        resident((D_MODEL, D_FF)),
        resident((D_FF, D_MODEL)),
        pl.BlockSpec((1, D_MODEL), const),
    ]


def _out_ffn(x, yr, ya, wo, g2, wg, wu, wd, gf, *, tm, final, ya_t):
    n = x.shape[0]
    return pl.pallas_call(
        functools.partial(_out_ffn_kernel, final=final, ya_t=ya_t),
        grid=(n // tm,),
        in_specs=_ffn_specs(tm, ya_t),
        out_specs=pl.BlockSpec((tm, D_MODEL), lambda i: (i, 0)),
        out_shape=jax.ShapeDtypeStruct((n, D_MODEL), F32),
        compiler_params=_cparams(1),
        name="out_ffn",
    )(x, yr, ya, wo, g2.reshape(1, D_MODEL), wg, wu, wd, gf.reshape(1, D_MODEL))


def _out_ffn_att(x, yr, ya, wo, g2, wg, wu, wd, gf, qt, knt, vnt, kt, vt, layer, *, tm, final):
    n = x.shape[0]
    steps = n // tm
    b = qt.shape[1]
    past = kt.shape[-1]
    spp = b // steps
    assert b % steps == 0 and spp == len(FF_CHUNKS) and spp % 2 == 0
    col_spec = pl.BlockSpec((W_ATT, b), lambda i: (0, 0))
    return pl.pallas_call(
        functools.partial(_out_ffn_att_kernel, final=final, layer=layer, spp=spp),
        grid=(steps,),
        in_specs=_ffn_specs(tm) + [
            col_spec, col_spec, col_spec,
            pl.BlockSpec((1, past), lambda i: (0, 0)),
            pl.BlockSpec(memory_space=pl.ANY),
            pl.BlockSpec(memory_space=pl.ANY),
        ],
        out_specs=[pl.BlockSpec((tm, D_MODEL), lambda i: (i, 0)), col_spec],
        out_shape=[jax.ShapeDtypeStruct((n, D_MODEL), F32), jax.ShapeDtypeStruct((W_ATT, b), F32)],
        scratch_shapes=[pltpu.VMEM((2, W_ATT, past), F32), pltpu.VMEM((2, W_ATT, past), F32),
                        pltpu.SemaphoreType.DMA((2, 2))],
        compiler_params=_cparams(1),
        name="out_ffn_att",
    )(x, yr, ya, wo, g2.reshape(1, D_MODEL), wg, wu, wd, gf.reshape(1, D_MODEL),
      qt, knt, vnt, _att_counts(past), kt, vt)


def kernel(x_prompt, x_sample, state_ret, cache_k, cache_v, norm1_g, w_in, w_gn, w_o, norm2_g,
           w_gate, w_up, w_down, final_g):
    batch, seq, _ = x_prompt.shape
    dec_batch, dec_seq, _ = x_sample.shape
    depth = w_in.shape[0]
    past = cache_k.shape[2]
    assert dec_seq == 1 and past == WINDOW_MAX
    assert all(seq % (Q_TILE * d) == 0 and seq // (Q_TILE * d) >= 2 for _, d in BRANCHES)
    keep = min(WINDOW_MAX, seq)

    tm = 512
    pos_p = jnp.arange(seq, dtype=F32)
    pos_s = jnp.full((dec_batch,), float(past), F32) + jnp.arange(dec_seq, dtype=F32)[0]
    tabs_p = _ret_tables(pos_p) + _rope_tables(pos_p)
    tabs_s = _ret_tables(pos_s) + _rope_tables(pos_s)

    w_in_b = w_in.astype(BF16)
    w_o_b = w_o.astype(BF16)
    w_gate_b = w_gate.astype(BF16)
    w_up_b = w_up.astype(BF16)
    w_down_b = w_down.astype(BF16)
    kt = jnp.transpose(cache_k, (0, 1, 3, 4, 2)).reshape(depth, dec_batch, W_ATT, past)
    vt = jnp.transpose(cache_v, (0, 1, 3, 4, 2)).reshape(depth, dec_batch, W_ATT, past)

    xp = x_prompt.reshape(batch * seq, D_MODEL)
    xs = x_sample.reshape(dec_batch, D_MODEL)
    st_p, ks_l, vs_l = [], [], []
    caches_p = (jnp.zeros((depth, batch, W_ATT, keep), F32),) * 2
    state_s = jnp.zeros((depth, dec_batch, W_RET_QK, DV_RET), F32)
    state_in = state_ret.reshape(depth, dec_batch, W_RET_QK, DV_RET)
    for l in range(depth):
        final = l == depth - 1
        ffn_w = (w_o_b[l], norm2_g[l], w_gate_b[l], w_up_b[l], w_down_b[l], final_g)

        (qr, kr, vr, gr), att, caches_p = _in_proj_prompt(
            xp, norm1_g[l], w_in_b[l], tabs_p, batch, seq, l, depth, caches_p, tm=tm)
        y_ret, r_p = _ret_prompt(qr, kr, vr, gr, w_gn[l], batch, seq, tt=tm)
        y_att = _att_prompt(att, batch, seq, unroll=8)
        st_p.append(r_p.reshape(batch, H_RET, DK_RET, DV_RET))

        qr, kr, vr, gr, ka, va, qt, knt, vnt = _in_proj_sample(xs, norm1_g[l], w_in_b[l], tabs_s)
        y_ret_s, state_s = _ret_sample(qr, kr, vr, gr, state_in, w_gn[l], l, state_s, sb=16)
        xp, y_att_s = _out_ffn_att(xp, y_ret, y_att, *ffn_w, qt, knt, vnt, kt, vt, l, tm=tm, final=final)
        xs = _out_ffn(xs, y_ret_s, y_att_s, *ffn_w, tm=dec_batch, final=final, ya_t=True)
        ks_l.append(ka.reshape(dec_batch, dec_seq, H_ATT, HD_ATT))
        vs_l.append(va.reshape(dec_batch, dec_seq, H_ATT, HD_ATT))

    to_cache = lambda c: jnp.transpose(c.reshape(depth, batch, H_ATT, HD_ATT, keep), (0, 1, 4, 2, 3))
    return (
        xp.reshape(batch, seq, D_MODEL),
        xs.reshape(dec_batch, dec_seq, D_MODEL),
        jnp.stack(st_p), to_cache(caches_p[0]), to_cache(caches_p[1]),
        state_s.reshape(depth, dec_batch, H_RET, DK_RET, DV_RET), jnp.stack(ks_l), jnp.stack(vs_l),
    )
```
